```python
import math
import jax, jax.numpy as jnp
from jax import lax
import numpy as np

D_MODEL = 2048
BATCH = 2
SEQ = 4096
DEPTH = 4
DEC_BATCH = 8
DEC_SEQ = 8
PAST_LEN = 16384
PAGE_SIZE = 128

HEAD_DIM = 64
RET_W = D_MODEL // 4
ATT_W = D_MODEL // 2
RWKV_W = D_MODEL - RET_W - ATT_W
MIX_W = RET_W + ATT_W + RWKV_W
RET_HEADS = RET_W // HEAD_DIM
ATT_HEADS = ATT_W // HEAD_DIM
ATT_KV_HEADS = 4
ATT_KV_W = ATT_KV_HEADS * HEAD_DIM
RWKV_HEADS = RWKV_W // HEAD_DIM
IDX_HEADS = 16
IDX_DIM = 64
IDX_SCALE = IDX_DIM ** -0.5 * IDX_HEADS ** -0.5
TOPK_MAX = 256
Q_BLOCK = 128
RET_CHUNK = 128
ROPE_THETA = 10000.0
DECAY_LORA = 96
AAA_LORA = 96
GATE_LORA = 256
RET_COLS = 4 * RET_W
ATT_COLS = ATT_W + 2 * ATT_KV_W + IDX_HEADS * IDX_DIM + IDX_DIM + IDX_HEADS
RWKV_COLS = 3 * RWKV_W + DECAY_LORA + AAA_LORA + GATE_LORA
IN_COLS = RET_COLS + ATT_COLS + RWKV_COLS
D_FF = 5504
N_EXPERTS = 8
TOP_K_EXPERTS = 2
D_FF_EXPERT = 7168
N_DENSE = (DEPTH + 1) // 2
N_MOE = DEPTH // 2
EPS = 1e-6
GN_EPS = 1e-5
RWKV_GN_EPS = 64e-5
F32 = jnp.float32

kernel_name = 'hybrid_retention_dsa_rwkv7_decode_step'


def rmsnorm(x, g):
    xf = x.astype(F32)
    y = xf * lax.rsqrt(jnp.mean(xf * xf, axis=-1, keepdims=True) + EPS)
    return y.astype(x.dtype) * g


def modulate(x, g, shift, scale):
    return rmsnorm(x, g) * (1 + scale[:, None, :]) + shift[:, None, :]


def head_norm(x, g, eps):
    xf = x.astype(F32)
    xc = xf - jnp.mean(xf, axis=-1, keepdims=True)
    y = xc * lax.rsqrt(jnp.mean(xc * xc, axis=-1, keepdims=True) + eps)
    return y.reshape(x.shape[:2] + (-1,)) * g


def heads(x, n):
    return x.reshape(x.shape[:2] + (n, -1))


def rope(x, pos):
    half = x.shape[-1] // 2
    inv = ROPE_THETA ** (-jnp.arange(half, dtype=F32) / half)
    ang = pos.astype(F32)[:, None] * inv[None, :]
    cos = jnp.cos(ang)[:, None, :]
    sin = jnp.sin(ang)[:, None, :]
    x1 = x[..., :half].astype(F32)
    x2 = x[..., half:].astype(F32)
    return jnp.concatenate([x1 * cos - x2 * sin, x2 * cos + x1 * sin], axis=-1).astype(x.dtype)


def retention(q, k, v, s0):
    b, t, h, d = q.shape
    c = math.gcd(t, RET_CHUNK)
    n = t // c
    log_g = jnp.log(1.0 - 2.0 ** (-5.0 - jnp.arange(h, dtype=F32)))
    i = jnp.arange(c, dtype=F32)
    rel = i[:, None] - i[None, :]
    dmask = jnp.where(rel >= 0, jnp.exp(log_g[:, None, None] * jnp.maximum(rel, 0.0)), 0.0)
    q_dec = jnp.exp(log_g[None, :] * (i[:, None] + 1.0))
    k_dec = jnp.exp(log_g[None, :] * (c - 1.0 - i[:, None]))
    c_dec = jnp.exp(log_g * c)

    def chunks(a):
        return a.astype(F32).reshape(b, n, c, h, d).swapaxes(0, 1)

    def step(state, qkv):
        qc, kc, vc = qkv
        att = jnp.einsum('bihd,bjhd->bhij', qc, kc) * dmask[None]
        inner = jnp.einsum('bhij,bjhe->bihe', att, vc)
        cross = jnp.einsum('bihd,bhde->bihe', qc, state) * q_dec[None, :, :, None]
        state = state * c_dec[None, :, None, None] + jnp.einsum('bjhd,bjhe->bhde', kc * k_dec[None, :, :, None], vc)
        return state, inner + cross

    state, out = lax.scan(step, s0.astype(F32), (chunks(q), chunks(k), chunks(v)))
    return out.swapaxes(0, 1).reshape(b, t, h, d), state


def rwkv7(p, prev_row, s0, mu, w0, w2, a0, a2, g2, k_k, k_a, r_k, gn):
    b, t, _ = p.shape
    p_prev = jnp.concatenate([prev_row[:, None, :].astype(p.dtype), p[:, :-1]], axis=1)
    xs = p + mu * (p_prev - p)
    cuts = [RWKV_W, 2 * RWKV_W, 3 * RWKV_W, 3 * RWKV_W + DECAY_LORA, 3 * RWKV_W + DECAY_LORA + AAA_LORA]
    r, k, v, wd, ad, gd = jnp.split(xs, cuts, axis=-1)
    w_log = -jax.nn.softplus(-(w0 + jnp.tanh(wd) @ w2).astype(F32)) - 0.5
    decay = jnp.exp(-jnp.exp(w_log))
    a = jax.nn.sigmoid((a0 + ad @ a2).astype(F32))
    g = (jax.nn.sigmoid(gd) @ g2).astype(F32)
    kk = heads(k.astype(F32) * k_k, RWKV_HEADS)
    kk = kk * lax.rsqrt(jnp.sum(kk * kk, axis=-1, keepdims=True) + 1e-12)
    kmod = k.astype(F32) * (1.0 + (a - 1.0) * k_a)
    rh = heads(r.astype(F32), RWKV_HEADS)
    kh = heads(kmod, RWKV_HEADS)
    vh = heads(v.astype(F32), RWKV_HEADS)
    ah = heads(a, RWKV_HEADS)
    wh = heads(decay, RWKV_HEADS)

    def step(state, inp):
        r_t, w_t, k_t, v_t, kk_t, b_t = inp
        sa = jnp.einsum('bhvk,bhk->bhv', state, -kk_t)
        state = state * w_t[:, :, None, :] + sa[..., None] * b_t[:, :, None, :] + v_t[..., None] * k_t[:, :, None, :]
        return state, jnp.einsum('bhvk,bhk->bhv', state, r_t)

    seq = (rh, wh, kh, vh, kk, kk * ah)
    state, y = lax.scan(step, s0.astype(F32), tuple(jnp.moveaxis(z, 1, 0) for z in seq))
    y = jnp.moveaxis(y, 0, 1)
    bonus = (jnp.sum(rh * kh * r_k, axis=-1, keepdims=True) * vh).reshape(b, t, RWKV_W)
    out = (head_norm(y, gn, RWKV_GN_EPS) + bonus) * g
    return out.astype(p.dtype), state, p[:, -1]


def index_scores(qi, wi, ki, qpos, kpos):
    rel = jax.nn.relu(jnp.einsum('bqhd,bsd->bqhs', qi.astype(F32), ki.astype(F32)))
    score = jnp.einsum('bqhs,bqh->bqs', rel, wi.astype(F32)) * IDX_SCALE
    return jnp.where(kpos[None, None, :] <= qpos[None, :, None], score, -jnp.inf)


def attend_selected(q, ksel, vsel, valid):
    b, nq, h, d = q.shape
    qg = q.astype(F32).reshape(b, nq, ATT_KV_HEADS, h // ATT_KV_HEADS, d)
    logits = jnp.einsum('bqhgd,bqkhd->bqhgk', qg, ksel.astype(F32)) * d ** -0.5
    logits = jnp.where(valid[:, :, None, None, :], logits, -jnp.inf)
    probs = jax.nn.softmax(logits, axis=-1)
    return jnp.einsum('bqhgk,bqkhd->bqhgd', probs, vsel.astype(F32)).reshape(b, nq, h * d)


take_rows = jax.vmap(lambda rows, idx: rows[idx])


def dsa_prompt(q, k, v, qi, ki, wi, pos):
    b, t = q.shape[:2]
    topk = min(TOPK_MAX, t // 4)
    nb = t // Q_BLOCK

    def block(i):
        s = i * Q_BLOCK
        sl = lambda z: lax.dynamic_slice_in_dim(z, s, Q_BLOCK, axis=1)
        qpos = lax.dynamic_slice_in_dim(pos, s, Q_BLOCK)
        score = index_scores(sl(qi), sl(wi), ki, qpos, pos)
        _, idx = lax.top_k(score, topk)
        valid = idx <= qpos[None, :, None]
        return attend_selected(sl(q), take_rows(k, idx), take_rows(v, idx), valid)

    out = lax.map(block, jnp.arange(nb))
    return out.swapaxes(0, 1).reshape(b, t, -1).astype(q.dtype)


def dsa_sample(q, k, v, qi, ki, wi, pos, k_cache, v_cache, ki_cache, page_table):
    b, t = q.shape[:2]
    past = page_table.shape[1] * PAGE_SIZE
    topk = min(TOPK_MAX, (past + t) // 4)
    ki_past = ki_cache[page_table].reshape(b, past, IDX_DIM).astype(ki.dtype)
    ki_all = jnp.concatenate([ki_past, ki], axis=1)
    kpos = jnp.arange(past + t)
    score = index_scores(qi, wi, ki_all, pos, kpos)
    _, idx = lax.top_k(score, topk)
    valid = idx <= pos[None, :, None]
    past_idx = jnp.minimum(idx, past - 1)
    phys = take_rows(page_table, past_idx // PAGE_SIZE)
    slot = past_idx % PAGE_SIZE
    new_idx = jnp.clip(idx - past, 0, t - 1)
    is_new = (idx >= past)[..., None, None]
    ksel = jnp.where(is_new, take_rows(k, new_idx), k_cache[phys, slot].astype(k.dtype))
    vsel = jnp.where(is_new, take_rows(v, new_idx), v_cache[phys, slot].astype(v.dtype))
    return attend_selected(q, ksel, vsel, valid).astype(q.dtype)


def token_mixers(h, pos, l, W, ret_s0, rwkv_s0, shift0, attend):
    proj = h @ W['w_in'][l]
    ret_p, att_p, rwkv_p = jnp.split(proj, [RET_COLS, RET_COLS + ATT_COLS], axis=-1)
    rq, rk, rv, rg = jnp.split(ret_p, 4, axis=-1)
    rq = rope(heads(rq, RET_HEADS), pos)
    rk = rope(heads(rk, RET_HEADS), pos) * HEAD_DIM ** -0.5
    ro, ret_state = retention(rq, rk, heads(rv, RET_HEADS), ret_s0)
    ro = (jax.nn.silu(rg.astype(F32)) * head_norm(ro, W['ret_gn'][l], GN_EPS)).astype(h.dtype)
    c1 = ATT_W
    c2 = c1 + ATT_KV_W
    c3 = c2 + ATT_KV_W
    c4 = c3 + IDX_HEADS * IDX_DIM
    c5 = c4 + IDX_DIM
    q, k, v, qi, ki, wi = jnp.split(att_p, [c1, c2, c3, c4, c5], axis=-1)
    q = rope(heads(q, ATT_HEADS), pos)
    k = rope(heads(k, ATT_KV_HEADS), pos)
    v = heads(v, ATT_KV_HEADS)
    qi = rope(heads(qi, IDX_HEADS), pos)
    ki = rope(ki[:, :, None, :], pos)[:, :, 0]
    ao = attend(l, q, k, v, qi, ki, wi, pos)
    wo, rwkv_state, shift = rwkv7(rwkv_p, shift0, rwkv_s0, W['rwkv_mu'][l], W['rwkv_w0'][l], W['rwkv_w2'][l],
                                  W['rwkv_a0'][l], W['rwkv_a2'][l], W['rwkv_g2'][l], W['rwkv_kk'][l],
                                  W['rwkv_ka'][l], W['rwkv_rk'][l], W['rwkv_gn'][l])
    out = jnp.concatenate([ro, ao, wo], axis=-1) @ W['w_out'][l]
    return out, (k, v, ki, ret_state, rwkv_state, shift)


def swiglu(h, w1, w3, w2):
    return (jax.nn.silu(h @ w1) * (h @ w3)) @ w2


def moe(h, router_w, router_b, w1, w3, w2):
    logits = (h @ router_w).astype(F32) + router_b.astype(F32)
    top_val, top_idx = lax.top_k(logits, TOP_K_EXPERTS)
    gates = jax.nn.softmax(top_val, axis=-1)
    gate_full = jnp.sum(jax.nn.one_hot(top_idx, N_EXPERTS, dtype=F32) * gates[..., None], axis=-2)
    out = jnp.zeros_like(h)
    for e in range(N_EXPERTS):
        out = out + gate_full[..., e:e + 1].astype(h.dtype) * swiglu(h, w1[e], w3[e], w2[e])
    return out


def trunk(x, c, pos, ret_s0, rwkv_s0, shift0, attend, W):
    per_layer = []
    for l in range(DEPTH):
        mod = jax.nn.silu(c) @ W['w_ada'][l] + W['b_ada'][l]
        sh1, sc1, g1, sh2, sc2, g2 = jnp.split(mod, 6, axis=-1)
        h = modulate(x, W['norm_mix'][l], sh1, sc1)
        out, st = token_mixers(h, pos, l, W, ret_s0[l], rwkv_s0[l], shift0[l], attend)
        x = x + g1[:, None, :] * out
        h = modulate(x, W['norm_ffn'][l], sh2, sc2)
        j = l // 2
        if l % 2 == 0:
            f = swiglu(h, W['ffn_w1'][j], W['ffn_w3'][j], W['ffn_w2'][j])
        else:
            f = moe(h, W['router_w'][j], W['router_b'][j], W['moe_w1'][j], W['moe_w3'][j], W['moe_w2'][j])
        x = x + g2[:, None, :] * f
        per_layer.append(st)
    y = rmsnorm(x, W['norm_final'])
    stacked = [jnp.stack([st[i] for st in per_layer]) for i in range(6)]
    return y, stacked


def setup_inputs(seed: int = 0) -> dict:
    key = jax.random.key(seed)
    ks = iter(jax.random.split(key, 64))
    nrm = lambda shape, s: jax.random.normal(next(ks), shape, F32) * s
    n_pages = PAST_LEN // PAGE_SIZE
    n_used = DEC_BATCH * n_pages
    n_pool = n_used + n_used // 4
    page_table = jax.random.permutation(next(ks), n_pool)[:n_used].reshape(DEC_BATCH, n_pages).astype(jnp.int32)
    return {
        'x_prompt': nrm((BATCH, SEQ, D_MODEL), 1.0),
        'x_sample': nrm((DEC_BATCH, DEC_SEQ, D_MODEL), 1.0),
        'c_prompt': nrm((BATCH, D_MODEL), 1.0),
        'c_sample': nrm((DEC_BATCH, D_MODEL), 1.0),
        'cache_k': nrm((DEPTH, n_pool, PAGE_SIZE, ATT_KV_HEADS, HEAD_DIM), 1.0),
        'cache_v': nrm((DEPTH, n_pool, PAGE_SIZE, ATT_KV_HEADS, HEAD_DIM), 1.0),
        'cache_idx_k': nrm((DEPTH, n_pool, PAGE_SIZE, IDX_DIM), 1.0),
        'state_ret': nrm((DEPTH, DEC_BATCH, RET_HEADS, HEAD_DIM, HEAD_DIM), 0.5),
        'state_rwkv': nrm((DEPTH, DEC_BATCH, RWKV_HEADS, HEAD_DIM, HEAD_DIM), 0.1),
        'state_rwkv_shift': nrm((DEPTH, DEC_BATCH, RWKV_COLS), 1.0),
        'page_table': page_table,
        'w_ada': nrm((DEPTH, D_MODEL, 6 * D_MODEL), 0.2 * D_MODEL ** -0.5),
        'b_ada': nrm((DEPTH, 6 * D_MODEL), 0.02),
        'norm_mix': 1.0 + nrm((DEPTH, D_MODEL), 0.02),
        'norm_ffn': 1.0 + nrm((DEPTH, D_MODEL), 0.02),
        'w_in': nrm((DEPTH, D_MODEL, IN_COLS), D_MODEL ** -0.5),
        'w_out': nrm((DEPTH, MIX_W, D_MODEL), MIX_W ** -0.5),
        'ret_gn': 1.0 + nrm((DEPTH, RET_W), 0.02),
        'rwkv_mu': jax.random.uniform(next(ks), (DEPTH, RWKV_COLS), F32),
        'rwkv_w0': jax.random.uniform(next(ks), (DEPTH, RWKV_W), F32, -6.0, -1.0),
        'rwkv_w2': nrm((DEPTH, DECAY_LORA, RWKV_W), 0.1 * DECAY_LORA ** -0.5),
        'rwkv_a0': nrm((DEPTH, RWKV_W), 0.1),
        'rwkv_a2': nrm((DEPTH, AAA_LORA, RWKV_W), 0.1 * AAA_LORA ** -0.5),
        'rwkv_g2': nrm((DEPTH, GATE_LORA, RWKV_W), GATE_LORA ** -0.5),
        'rwkv_kk': 0.85 + nrm((DEPTH, RWKV_W), 0.02),
        'rwkv_ka': 1.0 + nrm((DEPTH, RWKV_W), 0.02),
        'rwkv_rk': nrm((DEPTH, RWKV_HEADS, HEAD_DIM), 0.1),
        'rwkv_gn': 1.0 + nrm((DEPTH, RWKV_W), 0.02),
        'ffn_w1': nrm((N_DENSE, D_MODEL, D_FF), D_MODEL ** -0.5),
        'ffn_w3': nrm((N_DENSE, D_MODEL, D_FF), D_MODEL ** -0.5),
        'ffn_w2': nrm((N_DENSE, D_FF, D_MODEL), D_FF ** -0.5),
        'router_w': nrm((N_MOE, D_MODEL, N_EXPERTS), D_MODEL ** -0.5),
        'router_b': nrm((N_MOE, N_EXPERTS), 0.01),
        'moe_w1': nrm((N_MOE, N_EXPERTS, D_MODEL, D_FF_EXPERT), D_MODEL ** -0.5),
        'moe_w3': nrm((N_MOE, N_EXPERTS, D_MODEL, D_FF_EXPERT), D_MODEL ** -0.5),
        'moe_w2': nrm((N_MOE, N_EXPERTS, D_FF_EXPERT, D_MODEL), D_FF_EXPERT ** -0.5),
        'norm_final': 1.0 + nrm((D_MODEL,), 0.02),
    }


def reference(x_prompt, x_sample, c_prompt, c_sample, cache_k, cache_v, cache_idx_k, state_ret, state_rwkv,
              state_rwkv_shift, page_table, w_ada, b_ada, norm_mix, norm_ffn, w_in, w_out, ret_gn, rwkv_mu,
              rwkv_w0, rwkv_w2, rwkv_a0, rwkv_a2, rwkv_g2, rwkv_kk, rwkv_ka, rwkv_rk, rwkv_gn, ffn_w1, ffn_w3,
              ffn_w2, router_w, router_b, moe_w1, moe_w3, moe_w2, norm_final):
    W = dict(w_ada=w_ada, b_ada=b_ada, norm_mix=norm_mix, norm_ffn=norm_ffn, w_in=w_in, w_out=w_out,
             ret_gn=ret_gn, rwkv_mu=rwkv_mu, rwkv_w0=rwkv_w0, rwkv_w2=rwkv_w2, rwkv_a0=rwkv_a0,
             rwkv_a2=rwkv_a2, rwkv_g2=rwkv_g2, rwkv_kk=rwkv_kk, rwkv_ka=rwkv_ka, rwkv_rk=rwkv_rk,
             rwkv_gn=rwkv_gn, ffn_w1=ffn_w1, ffn_w3=ffn_w3, ffn_w2=ffn_w2, router_w=router_w,
             router_b=router_b, moe_w1=moe_w1, moe_w3=moe_w3, moe_w2=moe_w2, norm_final=norm_final)
    b_p, t_p = x_prompt.shape[:2]
    b_s, t_s = x_sample.shape[:2]
    past = page_table.shape[1] * PAGE_SIZE
    pos_p = jnp.arange(t_p)
    pos_s = past + jnp.arange(t_s)

    ret0 = jnp.zeros((DEPTH, b_p, RET_HEADS, HEAD_DIM, HEAD_DIM), F32)
    rwkv0 = jnp.zeros((DEPTH, b_p, RWKV_HEADS, HEAD_DIM, HEAD_DIM), F32)
    shift0 = jnp.zeros((DEPTH, b_p, RWKV_COLS), x_prompt.dtype)

    def attend_prompt(l, q, k, v, qi, ki, wi, pos):
        return dsa_prompt(q, k, v, qi, ki, wi, pos)

    def attend_sample(l, q, k, v, qi, ki, wi, pos):
        return dsa_sample(q, k, v, qi, ki, wi, pos, cache_k[l], cache_v[l], cache_idx_k[l], page_table)

    y_prompt, st_p = trunk(x_prompt, c_prompt, pos_p, ret0, rwkv0, shift0, attend_prompt, W)
    y_sample, st_s = trunk(x_sample, c_sample, pos_s, state_ret, state_rwkv, state_rwkv_shift, attend_sample, W)
    k_prompt, v_prompt, idx_k_prompt, ret_prompt, rwkv_prompt, shift_prompt = st_p
    k_sample, v_sample, idx_k_sample, ret_sample, rwkv_sample, shift_sample = st_s
    return (y_prompt, y_sample, k_prompt, v_prompt, idx_k_prompt, ret_prompt, rwkv_prompt, shift_prompt,
            k_sample, v_sample, idx_k_sample, ret_sample, rwkv_sample, shift_sample)
```

```python
import functools
import math

import jax
import jax.numpy as jnp
from jax import lax
from jax.experimental import pallas as pl
from jax.experimental.pallas import tpu as pltpu

F32 = jnp.float32
BF16 = jnp.bfloat16
I32 = jnp.int32

HEAD_DIM = 64
RET_HEADS = 8
ATT_HEADS = 16
ATT_KV_HEADS = 4
RWKV_HEADS = 8
IDX_HEADS = 16
IDX_DIM = 64
RET_W = RET_HEADS * HEAD_DIM
ATT_W = ATT_HEADS * HEAD_DIM
ATT_KV_W = ATT_KV_HEADS * HEAD_DIM
RWKV_W = RWKV_HEADS * HEAD_DIM
IDX_SCALE = IDX_DIM ** -0.5 * IDX_HEADS ** -0.5
TOPK_MAX = 256
Q_BLOCK = 128
RET_CHUNK = 128
RWKV_CHUNK = 64
ROPE_THETA = 10000.0
DECAY_LORA = 96
AAA_LORA = 96
GATE_LORA = 256
PAGE_SIZE = 128
N_EXPERTS = 8
EPS = 1e-6
GN_EPS = 1e-5
RWKV_GN_EPS = 64e-5

LANES = 128
VMEM_LIMIT_BYTES = 56 * 1024 * 1024

INT_MIN = -2 ** 31
NEG_INF_KEY = INT_MIN + 0x7FFFFF
MASKED_LOGIT = -1e30

_NN = (((1,), (0,)), ((), ()))
_NT = (((1,), (1,)), ((), ()))
_TN = (((0,), (0,)), ((), ()))


def _cp(*sem):
    return pltpu.CompilerParams(dimension_semantics=sem, vmem_limit_bytes=VMEM_LIMIT_BYTES)


def _dot1(a, b, dims=_NN):
    return lax.dot_general(a.astype(BF16), b.astype(BF16), dims, preferred_element_type=F32)


def _split2(a):
    hi = a.astype(BF16)
    lo = (a - hi.astype(F32)).astype(BF16)
    return hi, lo


def _dot3(a, b, dims=_NN):
    ah, al = _split2(a)
    bh, bl = _split2(b)
    d = lambda x, y: lax.dot_general(x, y, dims, preferred_element_type=F32)
    return d(ah, bh) + (d(al, bh) + d(ah, bl))


def _silu(x):
    return x * jax.nn.sigmoid(x)


def _sort_key(s):
    bits = lax.bitcast_convert_type(s, I32)
    return jnp.where(bits < 0, bits ^ 0x7FFFFFFF, bits)


def _ada_kernel(c_ref, w_ref, b_ref, o_ref):
    a = _silu(c_ref[...]).astype(BF16)
    o_ref[...] = jnp.dot(a, w_ref[...].astype(BF16), preferred_element_type=F32) + b_ref[...]


def _ada_mod(c_all, w_ada, b_ada):
    nl, d, n6 = w_ada.shape
    r = c_all.shape[0]
    tn = 1024
    return pl.pallas_call(
        _ada_kernel,
        grid=(nl, n6 // tn),
        in_specs=[pl.BlockSpec((r, d), lambda l, j: (0, 0)),
                  pl.BlockSpec((None, d, tn), lambda l, j: (l, 0, j)),
                  pl.BlockSpec((None, 1, tn), lambda l, j: (l, 0, j))],
        out_specs=pl.BlockSpec((None, r, tn), lambda l, j: (l, 0, j)),
        out_shape=jax.ShapeDtypeStruct((nl, r, n6), F32),
        compiler_params=_cp("parallel", "parallel"),
        name="ada_mod",
    )(c_all, w_ada, b_ada.reshape(nl, 1, n6))


def _norm_rows(x_ref, g_ref, sh_ref, sc_ref):
    x = x_ref[...]
    y = x * lax.rsqrt(jnp.mean(x * x, axis=-1, keepdims=True) + EPS)
    return y * g_ref[...] * (1.0 + sc_ref[...]) + sh_ref[...]


def _norm_mod_kernel(x_ref, g_ref, sh_ref, sc_ref, o_ref):
    o_ref[...] = _norm_rows(x_ref, g_ref, sh_ref, sc_ref).astype(o_ref.dtype)


def _norm_route_kernel(x_ref, g_ref, sh_ref, sc_ref, rw_ref, rb_ref, o_ref, route_ref):
    h = _norm_rows(x_ref, g_ref, sh_ref, sc_ref)
    o_ref[...] = h
    logits = _dot3(h, rw_ref[...]) + rb_ref[...]
    lane = lax.broadcasted_iota(I32, logits.shape, 1)
    logits = jnp.where(lane < N_EXPERTS, logits, -jnp.inf)
    m1 = jnp.max(logits, axis=-1, keepdims=True)
    i1 = jnp.min(jnp.where(logits == m1, lane, LANES), axis=-1, keepdims=True)
    rest = jnp.where(lane == i1, -jnp.inf, logits)
    m2 = jnp.max(rest, axis=-1, keepdims=True)
    i2 = jnp.min(jnp.where(rest == m2, lane, LANES), axis=-1, keepdims=True)
    e2 = jnp.exp(m2 - m1)
    g1 = 1.0 / (1.0 + e2)
    g2 = e2 / (1.0 + e2)
    route_ref[...] = jnp.where(lane == 0, i1.astype(F32),
                     jnp.where(lane == 1, i2.astype(F32),
                     jnp.where(lane == 2, g1, jnp.where(lane == 3, g2, 0.0))))


def _norm_mod(x, g, shift, scale, out_dtype, router=None):
    b, t, d = x.shape
    tt = min(t, 512)
    grid = (b, t // tt)
    row_spec = pl.BlockSpec((None, tt, d), lambda i, j: (i, j, 0))
    vec_spec = pl.BlockSpec((1, d), lambda i, j: (0, 0))
    mod_spec = pl.BlockSpec((None, 1, d), lambda i, j: (i, 0, 0))
    args = [x, g.reshape(1, d), shift.reshape(b, 1, d), scale.reshape(b, 1, d)]
    if router is None:
        return pl.pallas_call(
            _norm_mod_kernel, grid=grid,
            in_specs=[row_spec, vec_spec, mod_spec, mod_spec],
            out_specs=row_spec,
            out_shape=jax.ShapeDtypeStruct((b, t, d), out_dtype),
            compiler_params=_cp("parallel", "parallel"), name="norm_mod",
        )(*args)
    rw, rb = router
    rw_pad = jnp.pad(rw, ((0, 0), (0, LANES - N_EXPERTS)))
    rb_pad = jnp.pad(rb, (0, LANES - N_EXPERTS)).reshape(1, LANES)
    return pl.pallas_call(
        _norm_route_kernel, grid=grid,
        in_specs=[row_spec, vec_spec, mod_spec, mod_spec,
                  pl.BlockSpec((d, LANES), lambda i, j: (0, 0)),
                  pl.BlockSpec((1, LANES), lambda i, j: (0, 0))],
        out_specs=[row_spec, pl.BlockSpec((None, tt, LANES), lambda i, j: (i, j, 0))],
        out_shape=[jax.ShapeDtypeStruct((b, t, d), F32), jax.ShapeDtypeStruct((b, t, LANES), F32)],
        compiler_params=_cp("parallel", "parallel"), name="norm_route",
    )(*args, rw_pad, rb_pad)


def _mm_kernel(a_ref, w_ref, o_ref):
    o_ref[...] = jnp.dot(a_ref[...], w_ref[...].astype(BF16), preferred_element_type=F32)


def _mm_res_kernel(a_ref, w_ref, x_ref, g_ref, o_ref):
    acc = jnp.dot(a_ref[...], w_ref[...].astype(BF16), preferred_element_type=F32)
    o_ref[...] = x_ref[...] + g_ref[...] * acc


def _gate_spec(gate, m, tm, tn):
    if gate.ndim == 3:
        rows = m // gate.shape[0]
        assert rows % tm == 0
        return pl.BlockSpec((None, 1, tn), lambda j, i: (i // (rows // tm), 0, j))
    return pl.BlockSpec((tm, tn), lambda j, i: (i, j))


def _matmul(a, w, lead, tm, tn, res=None, gate=None):
    m, k = a.shape
    n = w.shape[-1]
    tm = min(tm, m)
    tn = min(tn, n)
    nlead = len(lead)
    grid = (pl.cdiv(n, tn), m // tm)
    a_spec = pl.BlockSpec((tm, k), lambda j, i: (i, 0))
    w_spec = pl.BlockSpec((None,) * nlead + (k, tn), lambda j, i: lead + (0, j))
    o_spec = pl.BlockSpec((tm, tn), lambda j, i: (i, j))
    out_shape = jax.ShapeDtypeStruct((m, n), F32)
    if res is None:
        return pl.pallas_call(_mm_kernel, grid=grid, in_specs=[a_spec, w_spec], out_specs=o_spec,
                              out_shape=out_shape, compiler_params=_cp("parallel", "parallel"),
                              name="matmul")(a, w)
    return pl.pallas_call(_mm_res_kernel, grid=grid,
                          in_specs=[a_spec, w_spec, o_spec, _gate_spec(gate, m, tm, tn)],
                          out_specs=o_spec, out_shape=out_shape,
                          compiler_params=_cp("parallel", "parallel"), name="matmul_res")(a, w, res, gate)


def _ffn_up_kernel(a_ref, w1_ref, w3_ref, o_ref):
    a = a_ref[...]
    g = jnp.dot(a, w1_ref[...].astype(BF16), preferred_element_type=F32)
    u = jnp.dot(a, w3_ref[...].astype(BF16), preferred_element_type=F32)
    o_ref[...] = (_silu(g) * u).astype(BF16)


def _ffn_up(a, w1, w3, lead, tm, tf):
    m, k = a.shape
    f = w1.shape[-1]
    tm = min(tm, m)
    nlead = len(lead)
    w_spec = pl.BlockSpec((None,) * nlead + (k, tf), lambda j, i: lead + (0, j))
    return pl.pallas_call(
        _ffn_up_kernel, grid=(pl.cdiv(f, tf), m // tm),
        in_specs=[pl.BlockSpec((tm, k), lambda j, i: (i, 0)), w_spec, w_spec],
        out_specs=pl.BlockSpec((tm, tf), lambda j, i: (i, j)),
        out_shape=jax.ShapeDtypeStruct((m, f), BF16),
        compiler_params=_cp("parallel", "parallel"), name="ffn_up",
    )(a, w1, w3)


def _ret_kernel(q_ref, k_ref, v_ref, g_ref, s0_ref, dm_ref, qd_ref, kd_ref, cd_ref, gn_ref,
                o_ref, so_ref, s_scr, *, nh):
    c = pl.program_id(1)

    @pl.when(c == 0)
    def _():
        s_scr[...] = s0_ref[...]

    for h in range(nh):
        q = q_ref[h]
        k = k_ref[h]
        v = v_ref[h]
        s = s_scr[h]
        att = _dot3(q, k, _NT) * dm_ref[h]
        out = _dot3(att, v) + _dot3(q, s) * qd_ref[h]
        s_scr[h] = s * cd_ref[h] + _dot3(k * kd_ref[h], v, _TN)
        xc = out - jnp.mean(out, axis=-1, keepdims=True)
        y = xc * lax.rsqrt(jnp.mean(xc * xc, axis=-1, keepdims=True) + GN_EPS)
        o_ref[h] = _silu(g_ref[h]) * (y * gn_ref[h])

    @pl.when(c == pl.num_programs(1) - 1)
    def _():
        so_ref[...] = s_scr[...]


def _ret_tables(c_true, cp):
    hh = jnp.arange(RET_HEADS, dtype=F32)
    log_g = jnp.log(1.0 - 2.0 ** (-5.0 - hh))
    i = jnp.arange(cp, dtype=F32)
    rel = i[:, None] - i[None, :]
    dmask = jnp.where(rel >= 0, jnp.exp(log_g[:, None, None] * jnp.maximum(rel, 0.0)), 0.0)
    q_dec = jnp.exp(log_g[:, None] * (i[None, :] + 1.0))[..., None]
    k_dec = jnp.exp(log_g[:, None] * (c_true - 1.0 - i[None, :]))[..., None]
    c_dec = jnp.exp(log_g * c_true).reshape(RET_HEADS, 1, 1)
    return dmask, q_dec, k_dec, c_dec


def _retention(q, k, v, g, s0, gn, c_true, cp):
    b, nh, tp, d = q.shape
    dmask, q_dec, k_dec, c_dec = _ret_tables(c_true, cp)
    seq = pl.BlockSpec((None, nh, cp, d), lambda i, c: (i, 0, c, 0))
    st = pl.BlockSpec((None, nh, d, d), lambda i, c: (i, 0, 0, 0))
    full = lambda shape: pl.BlockSpec(shape, lambda i, c: (0,) * len(shape))
    return pl.pallas_call(
        functools.partial(_ret_kernel, nh=nh), grid=(b, tp // cp),
        in_specs=[seq, seq, seq, seq, st, full((nh, cp, cp)), full((nh, cp, 1)), full((nh, cp, 1)),
                  full((nh, 1, 1)), full((nh, 1, d))],
        out_specs=[seq, st],
        out_shape=[jax.ShapeDtypeStruct((b, nh, tp, d), F32), jax.ShapeDtypeStruct((b, nh, d, d), F32)],
        scratch_shapes=[pltpu.VMEM((nh, d, d), F32)],
        compiler_params=_cp("parallel", "arbitrary"), name="retention",
    )(q, k, v, g, s0, dmask, q_dec, k_dec, c_dec, gn.reshape(nh, 1, d))


def _rwkv_pre_kernel(p_ref, pp_ref, mu_ref, w0_ref, w2_ref, a0_ref, a2_ref, g2_ref,
                     r_ref, k_ref, v_ref, lw_ref, a_ref, g_ref):
    p = p_ref[...]
    xs = p + mu_ref[...] * (pp_ref[...] - p)
    w = RWKV_W
    r_ref[...] = xs[:, :w]
    k_ref[...] = xs[:, w:2 * w]
    v_ref[...] = xs[:, 2 * w:3 * w]
    tail = xs[:, 3 * w:]
    z = -(w0_ref[...] + _dot3(jnp.tanh(tail), w2_ref[...]))
    softplus = jnp.maximum(z, 0.0) + jnp.log(1.0 + jnp.exp(-jnp.abs(z)))
    lw_ref[...] = -jnp.exp(-softplus - 0.5)
    a_ref[...] = jax.nn.sigmoid(a0_ref[...] + _dot3(tail, a2_ref[...]))
    g_ref[...] = _dot3(jax.nn.sigmoid(tail), g2_ref[...])


def _rwkv_pre(p, p_prev, mu, w0, w2, a0, a2, g2):
    b, t, cols = p.shape
    w = RWKV_W
    tail = cols - 3 * w
    w2p = jnp.pad(w2, ((0, tail - DECAY_LORA), (0, 0)))
    a2p = jnp.pad(a2, ((DECAY_LORA, tail - DECAY_LORA - AAA_LORA), (0, 0)))
    g2p = jnp.pad(g2, ((DECAY_LORA + AAA_LORA, 0), (0, 0)))
    tt = min(t, 512)
    row = pl.BlockSpec((None, tt, cols), lambda i, j: (i, j, 0))
    out = pl.BlockSpec((None, tt, w), lambda i, j: (i, j, 0))
    full = lambda shape: pl.BlockSpec(shape, lambda i, j: (0,) * len(shape))
    osh = jax.ShapeDtypeStruct((b, t, w), F32)
    return pl.pallas_call(
        _rwkv_pre_kernel, grid=(b, t // tt),
        in_specs=[row, row, full((1, cols)), full((1, w)), full((tail, w)), full((1, w)),
                  full((tail, w)), full((tail, w))],
        out_specs=[out] * 6, out_shape=[osh] * 6,
        compiler_params=_cp("parallel", "parallel"), name="rwkv_pre",
    )(p, p_prev, mu.reshape(1, cols), w0.reshape(1, w), w2p, a0.reshape(1, w), a2p, g2p)


def _rwkv_kernel(r_ref, k_ref, v_ref, lw_ref, a_ref, g_ref, s0_ref, kkw_ref, kaw_ref, rkw_ref, gn_ref,
                 o_ref, so_ref, s_scr, *, nh, cs):
    c = pl.program_id(1)

    @pl.when(c == 0)
    def _():
        s_scr[...] = s0_ref[...]

    row = lax.broadcasted_iota(I32, (cs, cs), 0)
    col = lax.broadcasted_iota(I32, (cs, cs), 1)
    incl = row >= col
    strict = row > col
    ones_tril = jnp.where(incl, 1.0, 0.0).astype(BF16)
    eye = jnp.where(row == col, 1.0, 0.0)

    for h in range(nh):
        r = r_ref[h]
        k = k_ref[h]
        v = v_ref[h]
        lw = lw_ref[h]
        a = a_ref[h]
        kk = k * kkw_ref[h]
        kk = kk * lax.rsqrt(jnp.sum(kk * kk, axis=-1, keepdims=True) + 1e-12)
        kmod = k * (1.0 + (a - 1.0) * kaw_ref[h])
        b = kk * a
        l1 = lw.astype(BF16)
        rem = lw - l1.astype(F32)
        l2 = rem.astype(BF16)
        l3 = (rem - l2.astype(F32)).astype(BF16)
        dd = lambda x: jnp.dot(ones_tril, x, preferred_element_type=F32)
        cum = dd(l1) + (dd(l2) + dd(l3))
        cum_end = cum[cs - 1:cs, :]
        e_inv = jnp.exp(-cum)
        e_rem = jnp.exp(cum_end - cum)
        a_t = -kk * jnp.exp(cum - lw)
        r_t = r * jnp.exp(cum)
        ar = jnp.concatenate([a_t, r_t], axis=0)
        mb = _dot3(ar, b * e_inv, _NT)
        mk = _dot3(ar, kmod * e_inv, _NT)
        l_ab = jnp.where(strict, mb[:cs], 0.0)
        p_rb = jnp.where(incl, mb[cs:], 0.0)
        l_ak = jnp.where(strict, mk[:cs], 0.0)
        p_rk = jnp.where(incl, mk[cs:], 0.0)
        pw = l_ab
        tinv = eye + l_ab
        for _ in range(int(math.log2(cs)) - 1):
            pw = _dot3(pw, pw)
            tinv = tinv + _dot3(tinv, pw)
        s0 = s_scr[h]
        hs = _dot3(ar, s0, _NT)
        u = _dot3(tinv, hs[:cs] + _dot3(l_ak, v))
        y = hs[cs:] + _dot3(p_rb, u) + _dot3(p_rk, v)
        s_scr[h] = s0 * jnp.exp(cum_end) + _dot3(u, b * e_rem, _TN) + _dot3(v, kmod * e_rem, _TN)
        xc = y - jnp.mean(y, axis=-1, keepdims=True)
        yn = xc * lax.rsqrt(jnp.mean(xc * xc, axis=-1, keepdims=True) + RWKV_GN_EPS)
        bonus = jnp.sum(r * kmod * rkw_ref[h], axis=-1, keepdims=True) * v
        o_ref[h] = (yn * gn_ref[h] + bonus) * g_ref[h]

    @pl.when(c == pl.num_programs(1) - 1)
    def _():
        so_ref[...] = s_scr[...]


def _rwkv_scan(r, k, v, lw, a, g, s0, kkw, kaw, rkw, gn, cs):
    b, nh, tp, d = r.shape
    seq = pl.BlockSpec((None, nh, cs, d), lambda i, c: (i, 0, c, 0))
    st = pl.BlockSpec((None, nh, d, d), lambda i, c: (i, 0, 0, 0))
    par = pl.BlockSpec((nh, 1, d), lambda i, c: (0, 0, 0))
    return pl.pallas_call(
        functools.partial(_rwkv_kernel, nh=nh, cs=cs), grid=(b, tp // cs),
        in_specs=[seq] * 6 + [st] + [par] * 4,
        out_specs=[seq, st],
        out_shape=[jax.ShapeDtypeStruct((b, nh, tp, d), F32), jax.ShapeDtypeStruct((b, nh, d, d), F32)],
        scratch_shapes=[pltpu.VMEM((nh, d, d), F32)],
        compiler_params=_cp("parallel", "arbitrary"), name="rwkv_scan",
    )(r, k, v, lw, a, g, s0, kkw.reshape(nh, 1, d), kaw.reshape(nh, 1, d), rkw.reshape(nh, 1, d),
      gn.reshape(nh, 1, d))


def _kth_largest(count_ge, shape, topk):
    def bit_body(it, res):
        cand = res | lax.shift_left(jnp.int32(1), 31 - it)
        cnt = count_ge(cand ^ INT_MIN)
        return jnp.where(cnt >= topk, cand, res)

    res = lax.fori_loop(0, 32, bit_body, jnp.zeros(shape, I32))
    return res ^ INT_MIN


def _dsa_prompt_kernel(qi_ref, wi_ref, kit_ref, q_ref, kt_ref, v_ref, o_ref,
                       key_scr, m_scr, l_scr, acc_scr, *, ch, topk):
    i = pl.program_id(1)
    nq = Q_BLOCK
    nch = (i * nq + nq + ch - 1) // ch
    qpos = i * nq + lax.broadcasted_iota(I32, (nq, ch), 0)
    lane = lax.broadcasted_iota(I32, (nq, ch), 1)
    wi = wi_ref[...]

    def score_body(j, carry):
        kit = kit_ref[j]
        s = jnp.zeros((nq, ch), F32)
        for h in range(IDX_HEADS):
            rel = jnp.dot(qi_ref[h], kit, preferred_element_type=F32)
            s = s + wi[:, h:h + 1] * jnp.maximum(rel, 0.0)
        s = s * IDX_SCALE
        s = jnp.where(s == 0.0, 0.0, s)
        s = jnp.where(j * ch + lane <= qpos, s, -jnp.inf)
        key_scr[j] = _sort_key(s)
        return carry

    lax.fori_loop(0, nch, score_body, 0)

    def count_ge(t):
        def body(j, acc):
            m = jnp.where(key_scr[j] >= t, 1.0, 0.0)
            part = m[:, 0:LANES]
            for u in range(1, ch // LANES):
                part = part + m[:, u * LANES:(u + 1) * LANES]
            return acc + part

        acc = lax.fori_loop(0, nch, body, jnp.zeros((nq, LANES), F32))
        return jnp.sum(acc, axis=1, keepdims=True)

    thr = _kth_largest(count_ge, (nq, 1), float(topk))
    thr = jnp.maximum(thr, NEG_INF_KEY + 1)

    m_scr[...] = jnp.full(m_scr.shape, MASKED_LOGIT, F32)
    l_scr[...] = jnp.zeros(l_scr.shape, F32)
    acc_scr[...] = jnp.zeros(acc_scr.shape, F32)

    def att_body(j, carry):
        sel = key_scr[j] >= thr
        for g in range(ATT_KV_HEADS):
            kt = kt_ref[g, j]
            vv = v_ref[g, j]
            for hh in range(ATT_HEADS // ATT_KV_HEADS):
                h = g * (ATT_HEADS // ATT_KV_HEADS) + hh
                lg = jnp.dot(q_ref[h], kt, preferred_element_type=F32) * HEAD_DIM ** -0.5
                lg = jnp.where(sel, lg, MASKED_LOGIT)
                m_old = m_scr[h]
                m_new = jnp.maximum(m_old, jnp.max(lg, axis=-1, keepdims=True))
                alpha = jnp.exp(m_old - m_new)
                p = jnp.exp(lg - m_new)
                l_scr[h] = alpha * l_scr[h] + jnp.sum(p, axis=-1, keepdims=True)
                acc_scr[h] = alpha * acc_scr[h] + jnp.dot(p.astype(BF16), vv, preferred_element_type=F32)
                m_scr[h] = m_new
        return carry

    lax.fori_loop(0, nch, att_body, 0)
    for h in range(ATT_HEADS):
        o_ref[h] = acc_scr[h] / l_scr[h]


def _dsa_prompt(qi_h, wi, kit, q_h, kt, v_c, topk):
    b, nh, t, d = q_h.shape
    nc, ch = kit.shape[1], kit.shape[3]
    qspec = pl.BlockSpec((None, nh, Q_BLOCK, d), lambda i, j: (i, 0, j, 0))
    return pl.pallas_call(
        functools.partial(_dsa_prompt_kernel, ch=ch, topk=topk), grid=(b, t // Q_BLOCK),
        in_specs=[qspec,
                  pl.BlockSpec((None, Q_BLOCK, IDX_HEADS), lambda i, j: (i, j, 0)),
                  pl.BlockSpec((None, nc, IDX_DIM, ch), lambda i, j: (i, 0, 0, 0)),
                  qspec,
                  pl.BlockSpec((None, ATT_KV_HEADS, nc, d, ch), lambda i, j: (i, 0, 0, 0, 0)),
                  pl.BlockSpec((None, ATT_KV_HEADS, nc, ch, d), lambda i, j: (i, 0, 0, 0, 0))],
        out_specs=qspec,
        out_shape=jax.ShapeDtypeStruct((b, nh, t, d), F32),
        scratch_shapes=[pltpu.VMEM((nc, Q_BLOCK, ch), I32), pltpu.VMEM((nh, Q_BLOCK, 1), F32),
                        pltpu.VMEM((nh, Q_BLOCK, 1), F32), pltpu.VMEM((nh, Q_BLOCK, d), F32)],
        compiler_params=_cp("parallel", "arbitrary"), name="dsa_prompt",
    )(qi_h, wi, kit, q_h, kt, v_c)


def _dsa_s_score_kernel(pt_ref, qi_ref, w_ref, *rest, pg, ns, tq, past):
    kc_refs = rest[:pg]
    kn_ref = rest[pg]
    o_ref = rest[pg + 1]
    j = pl.program_id(1)
    qi = qi_ref[...]
    wcol = w_ref[...]
    qpos = past + lax.broadcasted_iota(I32, (tq, PAGE_SIZE), 0)
    lane = lax.broadcasted_iota(I32, (tq, PAGE_SIZE), 1)

    def page_keys(kpage, kbase):
        rel = jnp.maximum(_dot1(qi, kpage, _NT), 0.0) * wcol
        s = jnp.sum(rel.reshape(IDX_HEADS, tq, PAGE_SIZE), axis=0) * IDX_SCALE
        s = jnp.where(s == 0.0, 0.0, s)
        s = jnp.where(kbase + lane <= qpos, s, -jnp.inf)
        return _sort_key(s)

    @pl.when(j < ns)
    def _():
        for u in range(pg):
            o_ref[u] = page_keys(kc_refs[u][...], (j * pg + u) * PAGE_SIZE)

    @pl.when(j == ns)
    def _():
        o_ref[0] = page_keys(kn_ref[...], past)
        for u in range(1, pg):
            o_ref[u] = jnp.full((tq, PAGE_SIZE), NEG_INF_KEY, I32)


def _dsa_s_attn_kernel(pt_ref, keys_ref, q_ref, *rest, pg, ns, tq, topk):
    kc_refs = rest[:pg]
    vc_refs = rest[pg:2 * pg]
    kn_ref, vn_ref, o_ref, thr_scr, m_scr, l_scr, acc_scr = rest[2 * pg:]
    j = pl.program_id(1)

    @pl.when(j == 0)
    def _():
        def count_ge(t):
            m = jnp.where(keys_ref[...] >= t[None], 1.0, 0.0)
            return jnp.sum(jnp.sum(m, axis=0), axis=1, keepdims=True)

        thr = _kth_largest(count_ge, (tq, 1), float(topk))
        thr_scr[...] = jnp.broadcast_to(jnp.maximum(thr, NEG_INF_KEY + 1), thr_scr.shape)
        m_scr[...] = jnp.full(m_scr.shape, MASKED_LOGIT, F32)
        l_scr[...] = jnp.zeros(l_scr.shape, F32)
        acc_scr[...] = jnp.zeros(acc_scr.shape, F32)

    q = q_ref[...]
    thr = thr_scr[...]

    def page(slot, kp, vp):
        sel8 = keys_ref[slot] >= thr
        sel = jnp.broadcast_to(sel8[None], (ATT_HEADS, tq, PAGE_SIZE)).reshape(ATT_HEADS * tq, PAGE_SIZE)
        lg = _dot1(q, kp, _NT) * HEAD_DIM ** -0.5
        lg = jnp.where(sel, lg, MASKED_LOGIT)
        m_old = m_scr[...]
        m_new = jnp.maximum(m_old, jnp.max(lg, axis=-1, keepdims=True))
        alpha = jnp.exp(m_old - m_new)
        p = jnp.exp(lg - m_new)
        l_scr[...] = alpha * l_scr[...] + jnp.sum(p, axis=-1, keepdims=True)
        acc_scr[...] = alpha * acc_scr[...] + _dot1(p, vp)
        m_scr[...] = m_new

    @pl.when(j < ns)
    def _():
        for u in range(pg):
            page(j * pg + u, kc_refs[u][...], vc_refs[u][...])

    @pl.when(j == ns)
    def _():
        page(ns * pg, kn_ref[...], vn_ref[...])
        o_ref[...] = acc_scr[...] / l_scr[...]


def _dsa_sample(l, qi_rows, wcol, ki_new, q_bd, k_new, v_new, cache_ik, cache_k2, cache_v2, page_table, topk):
    b, rows, _ = qi_rows.shape
    tq = rows // IDX_HEADS
    npages = page_table.shape[1]
    pg = 8
    ns = npages // pg
    past = npages * PAGE_SIZE
    kvw = cache_k2.shape[-1]

    def cache_spec(u, width):
        return pl.BlockSpec((None, None, PAGE_SIZE, width),
                            lambda i, j, pt: (l, pt[i, jnp.minimum(j, ns - 1) * pg + u], 0, 0))

    per_b = lambda shape: pl.BlockSpec((None,) + shape, lambda i, j, pt: (i,) + (0,) * len(shape))
    keys = pl.pallas_call(
        functools.partial(_dsa_s_score_kernel, pg=pg, ns=ns, tq=tq, past=past),
        grid_spec=pltpu.PrefetchScalarGridSpec(
            num_scalar_prefetch=1, grid=(b, ns + 1),
            in_specs=[per_b((rows, IDX_DIM)), per_b((rows, 1))]
                     + [cache_spec(u, IDX_DIM) for u in range(pg)] + [per_b((PAGE_SIZE, IDX_DIM))],
            out_specs=pl.BlockSpec((None, pg, tq, PAGE_SIZE), lambda i, j, pt: (i, j, 0, 0))),
        out_shape=jax.ShapeDtypeStruct((b, (ns + 1) * pg, tq, PAGE_SIZE), I32),
        compiler_params=_cp("parallel", "arbitrary"), name="dsa_sample_score",
    )(page_table, qi_rows, wcol, *([cache_ik] * pg), ki_new)

    nslots = (ns + 1) * pg
    return pl.pallas_call(
        functools.partial(_dsa_s_attn_kernel, pg=pg, ns=ns, tq=tq, topk=topk),
        grid_spec=pltpu.PrefetchScalarGridSpec(
            num_scalar_prefetch=1, grid=(b, ns + 1),
            in_specs=[per_b((nslots, tq, PAGE_SIZE)), per_b((rows, kvw))]
                     + [cache_spec(u, kvw) for u in range(pg)] + [cache_spec(u, kvw) for u in range(pg)]
                     + [per_b((PAGE_SIZE, kvw)), per_b((PAGE_SIZE, kvw))],
            out_specs=per_b((rows, kvw)),
            scratch_shapes=[pltpu.VMEM((tq, PAGE_SIZE), I32), pltpu.VMEM((rows, 1), F32),
                            pltpu.VMEM((rows, 1), F32), pltpu.VMEM((rows, kvw), F32)]),
        out_shape=jax.ShapeDtypeStruct((b, rows, kvw), F32),
        compiler_params=_cp("parallel", "arbitrary"), name="dsa_sample_attn",
    )(page_table, keys, q_bd, *([cache_k2] * pg), *([cache_v2] * pg), k_new, v_new)


MOE_TM = 512
GATHER_ROWS = 256


def _row_copy(src_ref, row, buf, r, sem):
    return pltpu.make_async_copy(src_ref.at[pl.ds(row, 1), :], buf.at[pl.ds(r, 1), :], sem)


def _gather_kernel(idx_ref, src_ref, o_ref, buf, sem, *, nrows):
    base = pl.program_id(0) * nrows

    def start(r, carry):
        _row_copy(src_ref, idx_ref[base + r], buf, r, sem).start()
        return carry

    def wait(r, carry):
        _row_copy(src_ref, idx_ref[base + r], buf, r, sem).wait()
        return carry

    lax.fori_loop(0, nrows, start, 0)
    lax.fori_loop(0, nrows, wait, 0)
    o_ref[...] = buf[...].astype(o_ref.dtype)


def _gather_rows(src, idx, out_dtype):
    p = idx.shape[0]
    d = src.shape[1]
    nrows = GATHER_ROWS
    return pl.pallas_call(
        functools.partial(_gather_kernel, nrows=nrows),
        grid_spec=pltpu.PrefetchScalarGridSpec(
            num_scalar_prefetch=1, grid=(p // nrows,),
            in_specs=[pl.BlockSpec(memory_space=pl.ANY)],
            out_specs=pl.BlockSpec((nrows, d), lambda i, idx: (i, 0)),
            scratch_shapes=[pltpu.VMEM((nrows, d), src.dtype), pltpu.SemaphoreType.DMA]),
        out_shape=jax.ShapeDtypeStruct((p, d), out_dtype),
        compiler_params=_cp("arbitrary"), name="moe_gather",
    )(idx, src)


def _moe_up_kernel(te_ref, nu_ref, a_ref, w1_ref, w3_ref, o_ref):
    used = pl.program_id(1) < nu_ref[0]

    @pl.when(used)
    def _():
        a = a_ref[...]
        g = jnp.dot(a, w1_ref[...].astype(BF16), preferred_element_type=F32)
        u = jnp.dot(a, w3_ref[...].astype(BF16), preferred_element_type=F32)
        o_ref[...] = (_silu(g) * u).astype(BF16)

    @pl.when(jnp.logical_not(used))
    def _():
        o_ref[...] = jnp.zeros(o_ref.shape, BF16)


def _moe_down_kernel(te_ref, nu_ref, a_ref, w_ref, o_ref):
    @pl.when(pl.program_id(1) == 0)
    def _():
        o_ref[...] = jnp.zeros(o_ref.shape, F32)

    @pl.when(pl.program_id(0) < nu_ref[0])
    def _():
        o_ref[...] += jnp.dot(a_ref[...], w_ref[...].astype(BF16), preferred_element_type=F32)


def _combine_kernel(d1_ref, d2_ref, y_ref, x_ref, route_ref, gate_ref, o_ref, buf1, buf2, sem, *, nrows):
    base = pl.program_id(0) * nrows

    def start(r, carry):
        _row_copy(y_ref, d1_ref[base + r], buf1, r, sem).start()
        _row_copy(y_ref, d2_ref[base + r], buf2, r, sem).start()
        return carry

    def wait(r, carry):
        _row_copy(y_ref, d1_ref[base + r], buf1, r, sem).wait()
        _row_copy(y_ref, d2_ref[base + r], buf2, r, sem).wait()
        return carry

    lax.fori_loop(0, nrows, start, 0)
    lax.fori_loop(0, nrows, wait, 0)
    route = route_ref[...]
    f = route[:, 2:3] * buf1[...] + route[:, 3:4] * buf2[...]
    o_ref[...] = x_ref[...] + gate_ref[...] * f


def _moe(h, route, x, gate, w1, w3, w2, jm):
    n, d = h.shape
    f = w1.shape[-1]
    tm = MOE_TM
    n_tiles = (2 * n + N_EXPERTS * (tm - 1)) // tm
    p_max = n_tiles * tm

    e_pair = jnp.concatenate([route[:, 0], route[:, 1]]).astype(I32)
    order = jnp.argsort(e_pair, stable=True).astype(I32)
    sorted_e = e_pair[order]
    counts = jnp.sum((e_pair[:, None] == jnp.arange(N_EXPERTS, dtype=I32)[None, :]).astype(I32), axis=0)
    tiles_per = (counts + tm - 1) // tm
    tile_end = jnp.cumsum(tiles_per)
    pad_start = (tile_end - tiles_per) * tm
    sort_start = jnp.cumsum(counts) - counts
    dest_sorted = pad_start[sorted_e] + jnp.arange(2 * n, dtype=I32) - sort_start[sorted_e]
    src_token = jnp.zeros((p_max,), I32).at[dest_sorted].set(order % n)
    dest = jnp.zeros((2 * n,), I32).at[order].set(dest_sorted)
    n_used = tile_end[-1:].astype(I32)
    tile_expert = jnp.minimum(
        jnp.searchsorted(tile_end, jnp.arange(n_tiles, dtype=I32), side="right"), N_EXPERTS - 1).astype(I32)

    hs = _gather_rows(h, src_token, BF16)

    tf = 1024
    lim = lambda i, nu: jnp.minimum(i, nu[0] - 1)
    act = pl.pallas_call(
        _moe_up_kernel,
        grid_spec=pltpu.PrefetchScalarGridSpec(
            num_scalar_prefetch=2, grid=(f // tf, n_tiles),
            in_specs=[pl.BlockSpec((tm, d), lambda j, i, te, nu: (lim(i, nu), 0)),
                      pl.BlockSpec((None, None, d, tf), lambda j, i, te, nu: (jm, te[lim(i, nu)], 0, j)),
                      pl.BlockSpec((None, None, d, tf), lambda j, i, te, nu: (jm, te[lim(i, nu)], 0, j))],
            out_specs=pl.BlockSpec((tm, tf), lambda j, i, te, nu: (i, j))),
        out_shape=jax.ShapeDtypeStruct((p_max, f), BF16),
        compiler_params=_cp("arbitrary", "arbitrary"), name="moe_up",
    )(tile_expert, n_used, hs, w1, w3)

    tk = 1024
    nk = f // tk
    klim = lambda i, k, nu: jnp.where(i < nu[0], k, nk - 1)
    y = pl.pallas_call(
        _moe_down_kernel,
        grid_spec=pltpu.PrefetchScalarGridSpec(
            num_scalar_prefetch=2, grid=(n_tiles, nk),
            in_specs=[pl.BlockSpec((tm, tk), lambda i, k, te, nu: (lim(i, nu), klim(i, k, nu))),
                      pl.BlockSpec((None, None, tk, d),
                                   lambda i, k, te, nu: (jm, te[lim(i, nu)], klim(i, k, nu), 0))],
            out_specs=pl.BlockSpec((tm, d), lambda i, k, te, nu: (i, 0))),
        out_shape=jax.ShapeDtypeStruct((p_max, d), F32),
        compiler_params=_cp("arbitrary", "arbitrary"), name="moe_down",
    )(tile_expert, n_used, act, w2)

    nrows = min(GATHER_ROWS, n)
    if gate.ndim == 3:
        per = n // gate.shape[0] // nrows
        gate_spec = pl.BlockSpec((None, 1, d), lambda i, d1, d2: (i // per, 0, 0))
    else:
        gate_spec = pl.BlockSpec((nrows, d), lambda i, d1, d2: (i, 0))
    row_spec = pl.BlockSpec((nrows, d), lambda i, d1, d2: (i, 0))
    return pl.pallas_call(
        functools.partial(_combine_kernel, nrows=nrows),
        grid_spec=pltpu.PrefetchScalarGridSpec(
            num_scalar_prefetch=2, grid=(n // nrows,),
            in_specs=[pl.BlockSpec(memory_space=pl.ANY), row_spec,
                      pl.BlockSpec((nrows, LANES), lambda i, d1, d2: (i, 0)), gate_spec],
            out_specs=row_spec,
            scratch_shapes=[pltpu.VMEM((nrows, d), F32), pltpu.VMEM((nrows, d), F32),
                            pltpu.SemaphoreType.DMA]),
        out_shape=jax.ShapeDtypeStruct((n, d), F32),
        compiler_params=_cp("arbitrary"), name="moe_combine",
    )(dest[:n], dest[n:], y, x, route, gate)


def _rope_tables(pos):
    half = HEAD_DIM // 2
    inv = ROPE_THETA ** (-jnp.arange(half, dtype=F32) / half)
    ang = pos.astype(F32)[:, None] * inv[None, :]
    return jnp.cos(ang)[:, None, :], jnp.sin(ang)[:, None, :]


def _rope(x, cos, sin):
    half = x.shape[-1] // 2
    x1 = x[..., :half]
    x2 = x[..., half:]
    return jnp.concatenate([x1 * cos - x2 * sin, x2 * cos + x1 * sin], axis=-1)


def _heads_first(x, nh):
    b, t, _ = x.shape
    return x.reshape(b, t, nh, HEAD_DIM).transpose(0, 2, 1, 3)


def _heads_last(x):
    b, nh, t, d = x.shape
    return x.transpose(0, 2, 1, 3).reshape(b, t, nh * d)


def _pad_t(x, tp):
    t = x.shape[2]
    return x if t == tp else jnp.pad(x, ((0, 0), (0, 0), (0, tp - t), (0, 0)))


def _gate_arg(g, t, m):
    if t >= 256:
        return g[:, None, :]
    return jnp.repeat(g, t, axis=0)


def _trunk(x, mod, pos, ret_s0, rwkv_s0, shift0, W, sample=None):
    b, t, d = x.shape
    m = b * t
    cos, sin = _rope_tables(pos)
    ret_c = math.gcd(t, RET_CHUNK)
    ret_cp = RET_CHUNK
    t_ret = -(-t // ret_cp) * ret_cp
    t_rwkv = -(-t // RWKV_CHUNK) * RWKV_CHUNK
    rwkv_cols = 3 * RWKV_W + DECAY_LORA + AAA_LORA + GATE_LORA
    ret_cols = 4 * RET_W
    att_cols = ATT_W + 2 * ATT_KV_W + IDX_HEADS * IDX_DIM + IDX_DIM + IDX_HEADS
    per_layer = []
    depth = W["w_in"].shape[0]
    for l in range(depth):
        ml = mod[l]
        sh1, sc1, g1, sh2, sc2, g2 = [ml[:, i * d:(i + 1) * d] for i in range(6)]
        h = _norm_mod(x, W["norm_mix"][l], sh1, sc1, BF16)
        proj = _matmul(h.reshape(m, d), W["w_in"], (l,), 1024, 768).reshape(b, t, -1)

        rq = _rope(proj[..., 0:RET_W].reshape(b, t, RET_HEADS, HEAD_DIM), cos, sin)
        rk = _rope(proj[..., RET_W:2 * RET_W].reshape(b, t, RET_HEADS, HEAD_DIM), cos, sin) * HEAD_DIM ** -0.5
        rv = proj[..., 2 * RET_W:3 * RET_W]
        rg = proj[..., 3 * RET_W:4 * RET_W]
        hf = lambda z: _pad_t(z.transpose(0, 2, 1, 3), t_ret)
        ro, ret_state = _retention(hf(rq), hf(rk), _pad_t(_heads_first(rv, RET_HEADS), t_ret),
                                   _pad_t(_heads_first(rg, RET_HEADS), t_ret),
                                   ret_s0[l], W["ret_gn"][l], ret_c, ret_cp)
        ro = _heads_last(ro[:, :, :t])

        o = ret_cols
        q = _rope(proj[..., o:o + ATT_W].reshape(b, t, ATT_HEADS, HEAD_DIM), cos, sin)
        o += ATT_W
        k = _rope(proj[..., o:o + ATT_KV_W].reshape(b, t, ATT_KV_HEADS, HEAD_DIM), cos, sin)
        o += ATT_KV_W
        v = proj[..., o:o + ATT_KV_W].reshape(b, t, ATT_KV_HEADS, HEAD_DIM)
        o += ATT_KV_W
        qi = _rope(proj[..., o:o + IDX_HEADS * IDX_DIM].reshape(b, t, IDX_HEADS, IDX_DIM), cos, sin)
        o += IDX_HEADS * IDX_DIM
        ki = _rope(proj[..., o:o + IDX_DIM].reshape(b, t, 1, IDX_DIM), cos, sin)[:, :, 0]
        o += IDX_DIM
        wi = proj[..., o:o + IDX_HEADS]
        if sample is None:
            topk = min(TOPK_MAX, t // 4)
            ch = min(512, t)
            nc = t // ch
            kit = ki.astype(BF16).reshape(b, nc, ch, IDX_DIM).transpose(0, 1, 3, 2)
            kh = k.astype(BF16).transpose(0, 2, 1, 3).reshape(b, ATT_KV_HEADS, nc, ch, HEAD_DIM)
            vh = v.astype(BF16).transpose(0, 2, 1, 3).reshape(b, ATT_KV_HEADS, nc, ch, HEAD_DIM)
            ao = _dsa_prompt(qi.astype(BF16).transpose(0, 2, 1, 3), wi, kit,
                             q.astype(BF16).transpose(0, 2, 1, 3), kh.transpose(0, 1, 2, 4, 3), vh, topk)
            ao = _heads_last(ao)
        else:
            cache_ik, cache_k2, cache_v2, page_table = sample
            past = page_table.shape[1] * PAGE_SIZE
            topk = min(TOPK_MAX, (past + t) // 4)
            rows = ATT_HEADS * t
            qi_rows = qi.astype(BF16).transpose(0, 2, 1, 3).reshape(b, rows, IDX_DIM)
            wcol = wi.transpose(0, 2, 1).reshape(b, rows, 1)
            padp = lambda z: jnp.pad(z, ((0, 0), (0, PAGE_SIZE - t), (0, 0)))
            grp = ATT_HEADS // ATT_KV_HEADS
            qh = q.astype(BF16).transpose(0, 2, 1, 3).reshape(b, ATT_KV_HEADS, grp * t, HEAD_DIM)
            q_bd = (qh[:, :, :, None, :] * jnp.eye(ATT_KV_HEADS, dtype=BF16)[None, :, None, :, None]
                    ).reshape(b, rows, ATT_KV_W)
            acc = _dsa_sample(l, qi_rows, wcol, padp(ki), q_bd, padp(k.reshape(b, t, ATT_KV_W)),
                              padp(v.reshape(b, t, ATT_KV_W)), cache_ik, cache_k2, cache_v2, page_table, topk)
            acc = acc.reshape(b, ATT_KV_HEADS, grp, t, ATT_KV_HEADS, HEAD_DIM)
            ao = jnp.stack([acc[:, g_, :, :, g_, :] for g_ in range(ATT_KV_HEADS)], axis=1)
            ao = ao.reshape(b, ATT_HEADS, t, HEAD_DIM).transpose(0, 2, 1, 3).reshape(b, t, ATT_W)

        p = proj[..., ret_cols + att_cols:]
        p_prev = jnp.concatenate([shift0[l][:, None, :], p[:, :-1]], axis=1)
        rr, kk_, vv_, lw, aa, gg = _rwkv_pre(p, p_prev, W["rwkv_mu"][l], W["rwkv_w0"][l], W["rwkv_w2"][l],
                                             W["rwkv_a0"][l], W["rwkv_a2"][l], W["rwkv_g2"][l])
        hp = lambda z: _pad_t(_heads_first(z, RWKV_HEADS), t_rwkv)
        wo, rwkv_state = _rwkv_scan(hp(rr), hp(kk_), hp(vv_), hp(lw), hp(aa), hp(gg), rwkv_s0[l],
                                    W["rwkv_kk"][l], W["rwkv_ka"][l], W["rwkv_rk"][l], W["rwkv_gn"][l],
                                    RWKV_CHUNK)
        wo = _heads_last(wo[:, :, :t])

        mixed = jnp.concatenate([ro, ao, wo], axis=-1).astype(BF16).reshape(m, d)
        x2 = _matmul(mixed, W["w_out"], (l,), 1024, 512, res=x.reshape(m, d), gate=_gate_arg(g1, t, m))

        j = l // 2
        g2a = _gate_arg(g2, t, m)
        if l % 2 == 0:
            h2 = _norm_mod(x2.reshape(b, t, d), W["norm_ffn"][l], sh2, sc2, BF16).reshape(m, d)
            act = _ffn_up(h2, W["ffn_w1"], W["ffn_w3"], (j,), 1024, 512)
            x3 = _matmul(act, W["ffn_w2"], (j,), 512, 512, res=x2, gate=g2a)
        else:
            h2, route = _norm_mod(x2.reshape(b, t, d), W["norm_ffn"][l], sh2, sc2, F32,
                                  router=(W["router_w"][j], W["router_b"][j]))
            x3 = _moe(h2.reshape(m, d), route.reshape(m, LANES), x2, g2a,
                      W["moe_w1"], W["moe_w3"], W["moe_w2"], j)
        x = x3.reshape(b, t, d)
        per_layer.append((k, v, ki, ret_state, rwkv_state, p[:, -1]))
    zeros = jnp.zeros((b, d), F32)
    y = _norm_mod(x, W["norm_final"], zeros, zeros, F32)
    stacked = [jnp.stack([st[i] for st in per_layer]) for i in range(6)]
    return y, stacked


def kernel(x_prompt, x_sample, c_prompt, c_sample, cache_k, cache_v, cache_idx_k, state_ret, state_rwkv,
           state_rwkv_shift, page_table, w_ada, b_ada, norm_mix, norm_ffn, w_in, w_out, ret_gn, rwkv_mu,
           rwkv_w0, rwkv_w2, rwkv_a0, rwkv_a2, rwkv_g2, rwkv_kk, rwkv_ka, rwkv_rk, rwkv_gn, ffn_w1, ffn_w3,
           ffn_w2, router_w, router_b, moe_w1, moe_w3, moe_w2, norm_final):
    W = dict(norm_mix=norm_mix, norm_ffn=norm_ffn, w_in=w_in, w_out=w_out, ret_gn=ret_gn, rwkv_mu=rwkv_mu,
             rwkv_w0=rwkv_w0, rwkv_w2=rwkv_w2, rwkv_a0=rwkv_a0, rwkv_a2=rwkv_a2, rwkv_g2=rwkv_g2,
             rwkv_kk=rwkv_kk, rwkv_ka=rwkv_ka, rwkv_rk=rwkv_rk, rwkv_gn=rwkv_gn, ffn_w1=ffn_w1,
             ffn_w3=ffn_w3, ffn_w2=ffn_w2, router_w=router_w, router_b=router_b, moe_w1=moe_w1,
             moe_w3=moe_w3, moe_w2=moe_w2, norm_final=norm_final)
    b_p, t_p, d = x_prompt.shape
    b_s, t_s, _ = x_sample.shape
    depth = w_in.shape[0]
    npool = cache_k.shape[1]
    past = page_table.shape[1] * PAGE_SIZE

    rows = -(-(b_p + b_s) // 8) * 8
    c_all = jnp.concatenate([c_prompt, c_sample, jnp.zeros((rows - b_p - b_s, d), F32)], axis=0)
    mod = _ada_mod(c_all, w_ada, b_ada)
    mod_p = mod[:, :b_p]
    mod_s = mod[:, b_p:b_p + b_s]

    ret0 = jnp.zeros((depth, b_p, RET_HEADS, HEAD_DIM, HEAD_DIM), F32)
    rwkv0 = jnp.zeros((depth, b_p, RWKV_HEADS, HEAD_DIM, HEAD_DIM), F32)
    shift0 = jnp.zeros((depth, b_p, state_rwkv_shift.shape[-1]), F32)
    y_p, st_p = _trunk(x_prompt, mod_p, jnp.arange(t_p), ret0, rwkv0, shift0, W)

    sample = (cache_idx_k, cache_k.reshape(depth, npool, PAGE_SIZE, ATT_KV_W),
              cache_v.reshape(depth, npool, PAGE_SIZE, ATT_KV_W), page_table)
    y_s, st_s = _trunk(x_sample, mod_s, past + jnp.arange(t_s), state_ret, state_rwkv, state_rwkv_shift, W,
                       sample=sample)
    return (y_p, y_s, *st_p, *st_s)
```

```python
import functools
import math

import jax
import jax.numpy as jnp
from jax import lax
from jax.experimental import pallas as pl
from jax.experimental.pallas import tpu as pltpu

F32 = jnp.float32
BF16 = jnp.bfloat16
I32 = jnp.int32

HEAD_DIM = 64
RET_HEADS = 8
ATT_HEADS = 16
ATT_KV_HEADS = 4
RWKV_HEADS = 8
IDX_HEADS = 16
IDX_DIM = 64
RET_W = RET_HEADS * HEAD_DIM
ATT_W = ATT_HEADS * HEAD_DIM
ATT_KV_W = ATT_KV_HEADS * HEAD_DIM
RWKV_W = RWKV_HEADS * HEAD_DIM
IDX_SCALE = IDX_DIM ** -0.5 * IDX_HEADS ** -0.5
TOPK_MAX = 256
Q_BLOCK = 128
RET_CHUNK = 128
RWKV_CHUNK = 64
ROPE_THETA = 10000.0
DECAY_LORA = 96
AAA_LORA = 96
GATE_LORA = 256
PAGE_SIZE = 128
N_EXPERTS = 8
EPS = 1e-6
GN_EPS = 1e-5
RWKV_GN_EPS = 64e-5

LANES = 128
VMEM_LIMIT_BYTES = 56 * 1024 * 1024

INT_MIN = -2 ** 31
NEG_INF_KEY = INT_MIN + 0x7FFFFF
MASKED_LOGIT = -1e30

_NN = (((1,), (0,)), ((), ()))
_NT = (((1,), (1,)), ((), ()))
_TN = (((0,), (0,)), ((), ()))
_BNN = (((2,), (1,)), ((0,), (0,)))
_BNT = (((2,), (2,)), ((0,), (0,)))


def _cp(*sem):
    return pltpu.CompilerParams(dimension_semantics=sem, vmem_limit_bytes=VMEM_LIMIT_BYTES)


def _dot1(a, b, dims=_NN):
    return lax.dot_general(a.astype(BF16), b.astype(BF16), dims, preferred_element_type=F32)


def _split2(a):
    hi = a.astype(BF16)
    lo = (a - hi.astype(F32)).astype(BF16)
    return hi, lo


def _dot3(a, b, dims=_NN):
    ah, al = _split2(a)
    bh, bl = _split2(b)
    d = lambda x, y: lax.dot_general(x, y, dims, preferred_element_type=F32)
    return d(ah, bh) + (d(al, bh) + d(ah, bl))


def _silu(x):
    return x * jax.nn.sigmoid(x)


def _split_heads(x, nh):
    return jnp.stack([x[:, h * HEAD_DIM:(h + 1) * HEAD_DIM] for h in range(nh)], axis=0)


def _merge_heads(x):
    return jnp.concatenate([x[h] for h in range(x.shape[0])], axis=-1)


def _sort_key(s):
    bits = lax.bitcast_convert_type(s, I32)
    return jnp.where(bits < 0, bits ^ 0x7FFFFFFF, bits)


def _ada_kernel(c_ref, w_ref, b_ref, o_ref):
    a = _silu(c_ref[...]).astype(BF16)
    o_ref[...] = jnp.dot(a, w_ref[...].astype(BF16), preferred_element_type=F32) + b_ref[...]


def _ada_mod(c_all, w_ada, b_ada):
    nl, d, n6 = w_ada.shape
    r = c_all.shape[0]
    tn = 1024
    return pl.pallas_call(
        _ada_kernel,
        grid=(nl, n6 // tn),
        in_specs=[pl.BlockSpec((r, d), lambda l, j: (0, 0)),
                  pl.BlockSpec((None, d, tn), lambda l, j: (l, 0, j)),
                  pl.BlockSpec((None, 1, tn), lambda l, j: (l, 0, j))],
        out_specs=pl.BlockSpec((None, r, tn), lambda l, j: (l, 0, j)),
        out_shape=jax.ShapeDtypeStruct((nl, r, n6), F32),
        compiler_params=_cp("parallel", "parallel"),
        name="ada_mod",
    )(c_all, w_ada, b_ada.reshape(nl, 1, n6))


def _norm_rows(x_ref, g_ref, sh_ref, sc_ref):
    x = x_ref[...]
    y = x * lax.rsqrt(jnp.mean(x * x, axis=-1, keepdims=True) + EPS)
    return y * g_ref[...] * (1.0 + sc_ref[...]) + sh_ref[...]


def _norm_mod_kernel(x_ref, g_ref, sh_ref, sc_ref, o_ref):
    o_ref[...] = _norm_rows(x_ref, g_ref, sh_ref, sc_ref).astype(o_ref.dtype)


def _norm_route_kernel(x_ref, g_ref, sh_ref, sc_ref, rw_ref, rb_ref, o_ref, route_ref):
    h = _norm_rows(x_ref, g_ref, sh_ref, sc_ref)
    o_ref[...] = h
    logits = _dot3(h, rw_ref[...]) + rb_ref[...]
    lane = lax.broadcasted_iota(I32, logits.shape, 1)
    logits = jnp.where(lane < N_EXPERTS, logits, -jnp.inf)
    m1 = jnp.max(logits, axis=-1, keepdims=True)
    i1 = jnp.min(jnp.where(logits == m1, lane, LANES), axis=-1, keepdims=True)
    rest = jnp.where(lane == i1, -jnp.inf, logits)
    m2 = jnp.max(rest, axis=-1, keepdims=True)
    i2 = jnp.min(jnp.where(rest == m2, lane, LANES), axis=-1, keepdims=True)
    e2 = jnp.exp(m2 - m1)
    g1 = 1.0 / (1.0 + e2)
    g2 = e2 / (1.0 + e2)
    route_ref[...] = jnp.where(lane == 0, i1.astype(F32),
                     jnp.where(lane == 1, i2.astype(F32),
                     jnp.where(lane == 2, g1, jnp.where(lane == 3, g2, 0.0))))


def _norm_mod(x, g, shift, scale, out_dtype, router=None):
    b, t, d = x.shape
    tt = min(t, 512)
    grid = (b, t // tt)
    row_spec = pl.BlockSpec((None, tt, d), lambda i, j: (i, j, 0))
    vec_spec = pl.BlockSpec((1, d), lambda i, j: (0, 0))
    mod_spec = pl.BlockSpec((None, 1, d), lambda i, j: (i, 0, 0))
    args = [x, g.reshape(1, d), shift.reshape(b, 1, d), scale.reshape(b, 1, d)]
    if router is None:
        return pl.pallas_call(
            _norm_mod_kernel, grid=grid,
            in_specs=[row_spec, vec_spec, mod_spec, mod_spec],
            out_specs=row_spec,
            out_shape=jax.ShapeDtypeStruct((b, t, d), out_dtype),
            compiler_params=_cp("parallel", "parallel"), name="norm_mod",
        )(*args)
    rw, rb = router
    rw_pad = jnp.pad(rw, ((0, 0), (0, LANES - N_EXPERTS)))
    rb_pad = jnp.pad(rb, (0, LANES - N_EXPERTS)).reshape(1, LANES)
    return pl.pallas_call(
        _norm_route_kernel, grid=grid,
        in_specs=[row_spec, vec_spec, mod_spec, mod_spec,
                  pl.BlockSpec((d, LANES), lambda i, j: (0, 0)),
                  pl.BlockSpec((1, LANES), lambda i, j: (0, 0))],
        out_specs=[row_spec, pl.BlockSpec((None, tt, LANES), lambda i, j: (i, j, 0))],
        out_shape=[jax.ShapeDtypeStruct((b, t, d), F32), jax.ShapeDtypeStruct((b, t, LANES), F32)],
        compiler_params=_cp("parallel", "parallel"), name="norm_route",
    )(*args, rw_pad, rb_pad)


def _mm_kernel(a_ref, w_ref, o_ref):
    o_ref[...] = jnp.dot(a_ref[...], w_ref[...].astype(BF16), preferred_element_type=F32)


def _mm_res_kernel(a_ref, w_ref, x_ref, g_ref, o_ref):
    acc = jnp.dot(a_ref[...], w_ref[...].astype(BF16), preferred_element_type=F32)
    o_ref[...] = x_ref[...] + g_ref[...] * acc


def _gate_spec(gate, m, tm, tn):
    if gate.ndim == 3:
        rows = m // gate.shape[0]
        assert rows % tm == 0
        return pl.BlockSpec((None, 1, tn), lambda j, i: (i // (rows // tm), 0, j))
    return pl.BlockSpec((tm, tn), lambda j, i: (i, j))


def _matmul(a, w, lead, tm, tn, res=None, gate=None):
    m, k = a.shape
    n = w.shape[-1]
    tm = min(tm, m)
    tn = min(tn, n)
    nlead = len(lead)
    grid = (pl.cdiv(n, tn), m // tm)
    a_spec = pl.BlockSpec((tm, k), lambda j, i: (i, 0))
    w_spec = pl.BlockSpec((None,) * nlead + (k, tn), lambda j, i: lead + (0, j))
    o_spec = pl.BlockSpec((tm, tn), lambda j, i: (i, j))
    out_shape = jax.ShapeDtypeStruct((m, n), F32)
    if res is None:
        return pl.pallas_call(_mm_kernel, grid=grid, in_specs=[a_spec, w_spec], out_specs=o_spec,
                              out_shape=out_shape, compiler_params=_cp("parallel", "parallel"),
                              name="matmul")(a, w)
    return pl.pallas_call(_mm_res_kernel, grid=grid,
                          in_specs=[a_spec, w_spec, o_spec, _gate_spec(gate, m, tm, tn)],
                          out_specs=o_spec, out_shape=out_shape,
                          compiler_params=_cp("parallel", "parallel"), name="matmul_res")(a, w, res, gate)


def _ffn_up_kernel(a_ref, w1_ref, w3_ref, o_ref):
    a = a_ref[...]
    g = jnp.dot(a, w1_ref[...].astype(BF16), preferred_element_type=F32)
    u = jnp.dot(a, w3_ref[...].astype(BF16), preferred_element_type=F32)
    o_ref[...] = (_silu(g) * u).astype(BF16)


def _ffn_up(a, w1, w3, lead, tm, tf):
    m, k = a.shape
    f = w1.shape[-1]
    tm = min(tm, m)
    nlead = len(lead)
    w_spec = pl.BlockSpec((None,) * nlead + (k, tf), lambda j, i: lead + (0, j))
    return pl.pallas_call(
        _ffn_up_kernel, grid=(pl.cdiv(f, tf), m // tm),
        in_specs=[pl.BlockSpec((tm, k), lambda j, i: (i, 0)), w_spec, w_spec],
        out_specs=pl.BlockSpec((tm, tf), lambda j, i: (i, j)),
        out_shape=jax.ShapeDtypeStruct((m, f), BF16),
        compiler_params=_cp("parallel", "parallel"), name="ffn_up",
    )(a, w1, w3)


def _ret_kernel(q_ref, k_ref, v_ref, g_ref, s0_ref, dm_ref, qd_ref, kd_ref, cd_ref, gn_ref,
                o_ref, so_ref, s_scr, *, nh):
    c = pl.program_id(1)

    @pl.when(c == 0)
    def _():
        s_scr[...] = s0_ref[...]

    q = _split_heads(q_ref[...], nh)
    k = _split_heads(k_ref[...], nh)
    v = _split_heads(v_ref[...], nh)
    s = s_scr[...]
    att = _dot3(q, k, _BNT) * dm_ref[...]
    out = _dot3(att, v, _BNN) + _dot3(q, s, _BNN) * qd_ref[...]
    kd = k * kd_ref[...]
    for h in range(nh):
        s_scr[h] = s[h] * cd_ref[h] + _dot3(kd[h], v[h], _TN)
    xc = out - jnp.mean(out, axis=-1, keepdims=True)
    y = xc * lax.rsqrt(jnp.mean(xc * xc, axis=-1, keepdims=True) + GN_EPS)
    o_ref[...] = _silu(g_ref[...]) * _merge_heads(y * gn_ref[...])

    @pl.when(c == pl.num_programs(1) - 1)
    def _():
        so_ref[...] = s_scr[...]


def _ret_tables(c_true, cp):
    hh = jnp.arange(RET_HEADS, dtype=F32)
    log_g = jnp.log(1.0 - 2.0 ** (-5.0 - hh))
    i = jnp.arange(cp, dtype=F32)
    rel = i[:, None] - i[None, :]
    dmask = jnp.where(rel >= 0, jnp.exp(log_g[:, None, None] * jnp.maximum(rel, 0.0)), 0.0)
    q_dec = jnp.exp(log_g[:, None] * (i[None, :] + 1.0))[..., None]
    k_dec = jnp.exp(log_g[:, None] * (c_true - 1.0 - i[None, :]))[..., None]
    c_dec = jnp.exp(log_g * c_true).reshape(RET_HEADS, 1, 1)
    return dmask, q_dec, k_dec, c_dec


def _retention(q, k, v, g, s0, gn, c_true, cp):
    b, tp, w = q.shape
    nh, d = RET_HEADS, HEAD_DIM
    dmask, q_dec, k_dec, c_dec = _ret_tables(c_true, cp)
    seq = pl.BlockSpec((None, cp, w), lambda i, c: (i, c, 0))
    st = pl.BlockSpec((None, nh, d, d), lambda i, c: (i, 0, 0, 0))
    full = lambda shape: pl.BlockSpec(shape, lambda i, c: (0,) * len(shape))
    return pl.pallas_call(
        functools.partial(_ret_kernel, nh=nh), grid=(b, tp // cp),
        in_specs=[seq, seq, seq, seq, st, full((nh, cp, cp)), full((nh, cp, 1)), full((nh, cp, 1)),
                  full((nh, 1, 1)), full((nh, 1, d))],
        out_specs=[seq, st],
        out_shape=[jax.ShapeDtypeStruct((b, tp, w), F32), jax.ShapeDtypeStruct((b, nh, d, d), F32)],
        scratch_shapes=[pltpu.VMEM((nh, d, d), F32)],
        compiler_params=_cp("parallel", "arbitrary"), name="retention",
    )(q, k, v, g, s0, dmask, q_dec, k_dec, c_dec, gn.reshape(nh, 1, d))


def _rwkv_pre_kernel(p_ref, pp_ref, mu_ref, w0_ref, w2_ref, a0_ref, a2_ref, g2_ref,
                     r_ref, k_ref, v_ref, lw_ref, a_ref, g_ref):
    p = p_ref[...]
    xs = p + mu_ref[...] * (pp_ref[...] - p)
    w = RWKV_W
    r_ref[...] = xs[:, :w]
    k_ref[...] = xs[:, w:2 * w]
    v_ref[...] = xs[:, 2 * w:3 * w]
    tail = xs[:, 3 * w:]
    z = -(w0_ref[...] + _dot3(jnp.tanh(tail), w2_ref[...]))
    softplus = jnp.maximum(z, 0.0) + jnp.log(1.0 + jnp.exp(-jnp.abs(z)))
    lw_ref[...] = -jnp.exp(-softplus - 0.5)
    a_ref[...] = jax.nn.sigmoid(a0_ref[...] + _dot3(tail, a2_ref[...]))
    g_ref[...] = _dot3(jax.nn.sigmoid(tail), g2_ref[...])


def _rwkv_pre(p, p_prev, mu, w0, w2, a0, a2, g2):
    b, t, cols = p.shape
    w = RWKV_W
    tail = cols - 3 * w
    w2p = jnp.pad(w2, ((0, tail - DECAY_LORA), (0, 0)))
    a2p = jnp.pad(a2, ((DECAY_LORA, tail - DECAY_LORA - AAA_LORA), (0, 0)))
    g2p = jnp.pad(g2, ((DECAY_LORA + AAA_LORA, 0), (0, 0)))
    tt = min(t, 512)
    row = pl.BlockSpec((None, tt, cols), lambda i, j: (i, j, 0))
    out = pl.BlockSpec((None, tt, w), lambda i, j: (i, j, 0))
    full = lambda shape: pl.BlockSpec(shape, lambda i, j: (0,) * len(shape))
    osh = jax.ShapeDtypeStruct((b, t, w), F32)
    return pl.pallas_call(
        _rwkv_pre_kernel, grid=(b, t // tt),
        in_specs=[row, row, full((1, cols)), full((1, w)), full((tail, w)), full((1, w)),
                  full((tail, w)), full((tail, w))],
        out_specs=[out] * 6, out_shape=[osh] * 6,
        compiler_params=_cp("parallel", "parallel"), name="rwkv_pre",
    )(p, p_prev, mu.reshape(1, cols), w0.reshape(1, w), w2p, a0.reshape(1, w), a2p, g2p)


def _rwkv_kernel(r_ref, k_ref, v_ref, lw_ref, a_ref, g_ref, s0_ref, kkw_ref, kaw_ref, rkw_ref, gn_ref,
                 o_ref, so_ref, s_scr, *, nh, cs):
    c = pl.program_id(1)

    @pl.when(c == 0)
    def _():
        s_scr[...] = s0_ref[...]

    row = lax.broadcasted_iota(I32, (cs, cs), 0)
    col = lax.broadcasted_iota(I32, (cs, cs), 1)
    incl = row >= col
    strict = row > col
    ones_tril = jnp.broadcast_to(jnp.where(incl, 1.0, 0.0).astype(BF16)[None], (nh, cs, cs))
    eye = jnp.where(row == col, 1.0, 0.0)

    r = _split_heads(r_ref[...], nh)
    k = _split_heads(k_ref[...], nh)
    v = _split_heads(v_ref[...], nh)
    lw = _split_heads(lw_ref[...], nh)
    a = _split_heads(a_ref[...], nh)
    kk = k * kkw_ref[...]
    kk = kk * lax.rsqrt(jnp.sum(kk * kk, axis=-1, keepdims=True) + 1e-12)
    kmod = k * (1.0 + (a - 1.0) * kaw_ref[...])
    b = kk * a
    l1 = lw.astype(BF16)
    rem = lw - l1.astype(F32)
    l2 = rem.astype(BF16)
    l3 = (rem - l2.astype(F32)).astype(BF16)
    dd = lambda x: lax.dot_general(ones_tril, x, _BNN, preferred_element_type=F32)
    cum = dd(l1) + (dd(l2) + dd(l3))
    cum_end = cum[:, cs - 1:cs, :]
    e_inv = jnp.exp(-cum)
    e_rem = jnp.exp(cum_end - cum)
    a_t = -kk * jnp.exp(cum - lw)
    r_t = r * jnp.exp(cum)
    ar = jnp.concatenate([a_t, r_t], axis=1)
    mb = _dot3(ar, b * e_inv, _BNT)
    mk = _dot3(ar, kmod * e_inv, _BNT)
    l_ab = jnp.where(strict, mb[:, :cs], 0.0)
    p_rb = jnp.where(incl, mb[:, cs:], 0.0)
    l_ak = jnp.where(strict, mk[:, :cs], 0.0)
    p_rk = jnp.where(incl, mk[:, cs:], 0.0)
    pw = l_ab
    tinv = eye + l_ab
    for _ in range(int(math.log2(cs)) - 1):
        pw = _dot3(pw, pw, _BNN)
        tinv = tinv + _dot3(tinv, pw, _BNN)
    s0 = s_scr[...]
    hs = _dot3(ar, s0, _BNT)
    u = _dot3(tinv, hs[:, :cs] + _dot3(l_ak, v, _BNN), _BNN)
    y = hs[:, cs:] + _dot3(p_rb, u, _BNN) + _dot3(p_rk, v, _BNN)
    uv = jnp.concatenate([u, v], axis=1)
    bk = jnp.concatenate([b * e_rem, kmod * e_rem], axis=1)
    e_end = jnp.exp(cum_end)
    for h in range(nh):
        s_scr[h] = s0[h] * e_end[h] + _dot3(uv[h], bk[h], _TN)
    xc = y - jnp.mean(y, axis=-1, keepdims=True)
    yn = xc * lax.rsqrt(jnp.mean(xc * xc, axis=-1, keepdims=True) + RWKV_GN_EPS)
    bonus = jnp.sum(r * kmod * rkw_ref[...], axis=-1, keepdims=True) * v
    o_ref[...] = _merge_heads(yn * gn_ref[...] + bonus) * g_ref[...]

    @pl.when(c == pl.num_programs(1) - 1)
    def _():
        so_ref[...] = s_scr[...]


def _rwkv_scan(r, k, v, lw, a, g, s0, kkw, kaw, rkw, gn, cs):
    b, tp, w = r.shape
    nh, d = RWKV_HEADS, HEAD_DIM
    seq = pl.BlockSpec((None, cs, w), lambda i, c: (i, c, 0))
    st = pl.BlockSpec((None, nh, d, d), lambda i, c: (i, 0, 0, 0))
    par = pl.BlockSpec((nh, 1, d), lambda i, c: (0, 0, 0))
    return pl.pallas_call(
        functools.partial(_rwkv_kernel, nh=nh, cs=cs), grid=(b, tp // cs),
        in_specs=[seq] * 6 + [st] + [par] * 4,
        out_specs=[seq, st],
        out_shape=[jax.ShapeDtypeStruct((b, tp, w), F32), jax.ShapeDtypeStruct((b, nh, d, d), F32)],
        scratch_shapes=[pltpu.VMEM((nh, d, d), F32)],
        compiler_params=_cp("parallel", "arbitrary"), name="rwkv_scan",
    )(r, k, v, lw, a, g, s0, kkw.reshape(nh, 1, d), kaw.reshape(nh, 1, d), rkw.reshape(nh, 1, d),
      gn.reshape(nh, 1, d))


def _kth_largest(count_ge, shape, topk):
    def bit_body(it, res):
        cand = res | lax.shift_left(jnp.int32(1), 31 - it)
        cnt = count_ge(cand ^ INT_MIN)
        return jnp.where(cnt >= topk, cand, res)

    res = lax.fori_loop(0, 32, bit_body, jnp.zeros(shape, I32))
    return res ^ INT_MIN


def _dsa_prompt_kernel(qi_ref, wi_ref, kit_ref, q_ref, kt_ref, v_ref, o_ref,
                       key_scr, bias_scr, lg_scr, m_scr, acc_scr, *, ch, topk):
    i = pl.program_id(1)
    nq = Q_BLOCK
    nch = (i * nq + nq + ch - 1) // ch
    qpos = i * nq + lax.broadcasted_iota(I32, (nq, ch), 0)
    lane = lax.broadcasted_iota(I32, (nq, ch), 1)
    wi = wi_ref[...]

    def score_body(j, carry):
        kit = kit_ref[j]
        s = jnp.zeros((nq, ch), F32)
        for h in range(IDX_HEADS):
            rel = jnp.dot(qi_ref[h], kit, preferred_element_type=F32)
            s = s + wi[:, h:h + 1] * jnp.maximum(rel, 0.0)
        s = s * IDX_SCALE
        s = jnp.where(s == 0.0, 0.0, s)
        s = jnp.where(j * ch + lane <= qpos, s, -jnp.inf)
        key_scr[j] = _sort_key(s)
        return carry

    lax.fori_loop(0, nch, score_body, 0)

    def count_ge(t):
        def body(j, acc):
            m = jnp.where(key_scr[j] >= t, 1.0, 0.0)
            part = m[:, 0:LANES]
            for u in range(1, ch // LANES):
                part = part + m[:, u * LANES:(u + 1) * LANES]
            return acc + part

        acc = lax.fori_loop(0, nch, body, jnp.zeros((nq, LANES), F32))
        return jnp.sum(acc, axis=1, keepdims=True)

    thr = _kth_largest(count_ge, (nq, 1), float(topk))
    thr = jnp.maximum(thr, NEG_INF_KEY + 1)

    def bias_body(j, carry):
        bias_scr[j] = jnp.where(key_scr[j] >= thr, 0.0, MASKED_LOGIT)
        return carry

    lax.fori_loop(0, nch, bias_body, 0)

    hpg = ATT_HEADS // ATT_KV_HEADS
    ntile = ch // LANES
    for g in range(ATT_KV_HEADS):
        m_scr[...] = jnp.full(m_scr.shape, MASKED_LOGIT, F32)
        acc_scr[...] = jnp.zeros(acc_scr.shape, F32)

        def logit_body(j, carry, g=g):
            kt = kt_ref[g, j]
            bias = bias_scr[j]
            for hh in range(hpg):
                lg = jnp.dot(q_ref[g * hpg + hh], kt, preferred_element_type=F32) + bias
                lg_scr[hh, j] = lg
                part = lg[:, 0:LANES]
                for u in range(1, ntile):
                    part = jnp.maximum(part, lg[:, u * LANES:(u + 1) * LANES])
                m_scr[hh] = jnp.maximum(m_scr[hh], part)
            return carry

        lax.fori_loop(0, nch, logit_body, 0)
        for hh in range(hpg):
            m_scr[hh] = jnp.broadcast_to(jnp.max(m_scr[hh], axis=-1, keepdims=True), (nq, LANES))

        def pv_body(j, carry, g=g):
            vv = v_ref[g, j]
            for hh in range(hpg):
                m = m_scr[hh]
                lg = lg_scr[hh, j]
                p = jnp.concatenate(
                    [jnp.exp(lg[:, u * LANES:(u + 1) * LANES] - m).astype(BF16) for u in range(ntile)], axis=1)
                acc_scr[hh] += jnp.dot(p, vv, preferred_element_type=F32)
            return carry

        lax.fori_loop(0, nch, pv_body, 0)
        for hh in range(hpg):
            acc = acc_scr[hh]
            o_ref[g * hpg + hh] = acc[:, :HEAD_DIM] / acc[:, HEAD_DIM:HEAD_DIM + 1]


def _dsa_prompt(qi_h, wi, kit, q_h, kt, v_c, topk):
    b, nh, t, d = q_h.shape
    nc, ch = kit.shape[1], kit.shape[3]
    hpg = ATT_HEADS // ATT_KV_HEADS
    qspec = pl.BlockSpec((None, nh, Q_BLOCK, d), lambda i, j: (i, 0, j, 0))
    return pl.pallas_call(
        functools.partial(_dsa_prompt_kernel, ch=ch, topk=topk), grid=(b, t // Q_BLOCK),
        in_specs=[qspec,
                  pl.BlockSpec((None, Q_BLOCK, IDX_HEADS), lambda i, j: (i, j, 0)),
                  pl.BlockSpec((None, nc, IDX_DIM, ch), lambda i, j: (i, 0, 0, 0)),
                  qspec,
                  pl.BlockSpec((None, ATT_KV_HEADS, nc, d, ch), lambda i, j: (i, 0, 0, 0, 0)),
                  pl.BlockSpec((None, ATT_KV_HEADS, nc, ch, LANES), lambda i, j: (i, 0, 0, 0, 0))],
        out_specs=qspec,
        out_shape=jax.ShapeDtypeStruct((b, nh, t, d), F32),
        scratch_shapes=[pltpu.VMEM((nc, Q_BLOCK, ch), I32), pltpu.VMEM((nc, Q_BLOCK, ch), F32),
                        pltpu.VMEM((hpg, nc, Q_BLOCK, ch), F32), pltpu.VMEM((hpg, Q_BLOCK, LANES), F32),
                        pltpu.VMEM((hpg, Q_BLOCK, LANES), F32)],
        compiler_params=_cp("parallel", "arbitrary"), name="dsa_prompt",
    )(qi_h, wi, kit, q_h, kt, v_c)


def _dsa_s_score_kernel(pt_ref, qi_ref, w_ref, *rest, pg, ns, tq, past):
    kc_refs = rest[:pg]
    kn_ref = rest[pg]
    o_ref = rest[pg + 1]
    j = pl.program_id(1)
    qi = qi_ref[...]
    wcol = w_ref[...]
    qpos = past + lax.broadcasted_iota(I32, (tq, PAGE_SIZE), 0)
    lane = lax.broadcasted_iota(I32, (tq, PAGE_SIZE), 1)

    def page_keys(kpage, kbase):
        rel = jnp.maximum(_dot1(qi, kpage, _NT), 0.0) * wcol
        s = jnp.sum(rel.reshape(IDX_HEADS, tq, PAGE_SIZE), axis=0) * IDX_SCALE
        s = jnp.where(s == 0.0, 0.0, s)
        s = jnp.where(kbase + lane <= qpos, s, -jnp.inf)
        return _sort_key(s)

    @pl.when(j < ns)
    def _():
        for u in range(pg):
            o_ref[u] = page_keys(kc_refs[u][...], (j * pg + u) * PAGE_SIZE)

    @pl.when(j == ns)
    def _():
        o_ref[0] = page_keys(kn_ref[...], past)
        for u in range(1, pg):
            o_ref[u] = jnp.full((tq, PAGE_SIZE), NEG_INF_KEY, I32)


def _dsa_s_attn_kernel(pt_ref, keys_ref, q_ref, *rest, pg, ns, tq, topk):
    kc_refs = rest[:pg]
    vc_refs = rest[pg:2 * pg]
    kn_ref, vn_ref, o_ref, thr_scr, m_scr, l_scr, acc_scr = rest[2 * pg:]
    j = pl.program_id(1)

    @pl.when(j == 0)
    def _():
        def count_ge(t):
            m = jnp.where(keys_ref[...] >= t[None], 1.0, 0.0)
            return jnp.sum(jnp.sum(m, axis=0), axis=1, keepdims=True)

        thr = _kth_largest(count_ge, (tq, 1), float(topk))
        thr_scr[...] = jnp.broadcast_to(jnp.maximum(thr, NEG_INF_KEY + 1), thr_scr.shape)
        m_scr[...] = jnp.full(m_scr.shape, MASKED_LOGIT, F32)
        l_scr[...] = jnp.zeros(l_scr.shape, F32)
        acc_scr[...] = jnp.zeros(acc_scr.shape, F32)

    q = q_ref[...]
    thr = thr_scr[...]

    def pages(slots, kps, vps):
        lgs = []
        for slot, kp in zip(slots, kps):
            bias8 = jnp.where(keys_ref[slot] >= thr, 0.0, MASKED_LOGIT)
            bias = jnp.broadcast_to(bias8[None], (ATT_HEADS, tq, PAGE_SIZE)).reshape(ATT_HEADS * tq, PAGE_SIZE)
            lgs.append(_dot1(q, kp, _NT) + bias)
        lg = lgs[0] if len(lgs) == 1 else jnp.concatenate(lgs, axis=1)
        m_old = m_scr[...]
        m_new = jnp.maximum(m_old, jnp.max(lg, axis=-1, keepdims=True))
        alpha = jnp.exp(m_old - m_new)
        p = jnp.exp(lg - m_new)
        l_scr[...] = alpha * l_scr[...] + jnp.sum(p, axis=-1, keepdims=True)
        acc = alpha * acc_scr[...]
        for n, vp in enumerate(vps):
            acc = acc + _dot1(p[:, n * PAGE_SIZE:(n + 1) * PAGE_SIZE], vp)
        acc_scr[...] = acc
        m_scr[...] = m_new

    def flat(ref):
        return jnp.concatenate([ref[:, g, :] for g in range(ATT_KV_HEADS)], axis=-1)

    @pl.when(j < ns)
    def _():
        pages([j * pg + u for u in range(pg)], [flat(r) for r in kc_refs], [flat(r) for r in vc_refs])

    @pl.when(j == ns)
    def _():
        pages([ns * pg], [kn_ref[...]], [vn_ref[...]])
        o_ref[...] = acc_scr[...] / l_scr[...]


def _dsa_sample(l, qi_rows, wcol, ki_new, q_bd, k_new, v_new, cache_ik, cache_k2, cache_v2, page_table, topk):
    b, rows, _ = qi_rows.shape
    tq = rows // IDX_HEADS
    npages = page_table.shape[1]
    pg = 8
    ns = npages // pg
    past = npages * PAGE_SIZE
    kvw = ATT_KV_W

    def cache_spec(u, width):
        return pl.BlockSpec((None, None, PAGE_SIZE, width),
                            lambda i, j, pt: (l, pt[i, jnp.minimum(j, ns - 1) * pg + u], 0, 0))

    def kv_spec(u):
        return pl.BlockSpec((None, None, PAGE_SIZE, ATT_KV_HEADS, HEAD_DIM),
                            lambda i, j, pt: (l, pt[i, jnp.minimum(j, ns - 1) * pg + u], 0, 0, 0))

    per_b = lambda shape: pl.BlockSpec((None,) + shape, lambda i, j, pt: (i,) + (0,) * len(shape))
    keys = pl.pallas_call(
        functools.partial(_dsa_s_score_kernel, pg=pg, ns=ns, tq=tq, past=past),
        grid_spec=pltpu.PrefetchScalarGridSpec(
            num_scalar_prefetch=1, grid=(b, ns + 1),
            in_specs=[per_b((rows, IDX_DIM)), per_b((rows, 1))]
                     + [cache_spec(u, IDX_DIM) for u in range(pg)] + [per_b((PAGE_SIZE, IDX_DIM))],
            out_specs=pl.BlockSpec((None, pg, tq, PAGE_SIZE), lambda i, j, pt: (i, j, 0, 0))),
        out_shape=jax.ShapeDtypeStruct((b, (ns + 1) * pg, tq, PAGE_SIZE), I32),
        compiler_params=_cp("parallel", "arbitrary"), name="dsa_sample_score",
    )(page_table, qi_rows, wcol, *([cache_ik] * pg), ki_new)

    nslots = (ns + 1) * pg
    return pl.pallas_call(
        functools.partial(_dsa_s_attn_kernel, pg=pg, ns=ns, tq=tq, topk=topk),
        grid_spec=pltpu.PrefetchScalarGridSpec(
            num_scalar_prefetch=1, grid=(b, ns + 1),
            in_specs=[per_b((nslots, tq, PAGE_SIZE)), per_b((rows, kvw))]
                     + [kv_spec(u) for u in range(pg)] + [kv_spec(u) for u in range(pg)]
                     + [per_b((PAGE_SIZE, kvw)), per_b((PAGE_SIZE, kvw))],
            out_specs=per_b((rows, kvw)),
            scratch_shapes=[pltpu.VMEM((tq, PAGE_SIZE), I32), pltpu.VMEM((rows, 1), F32),
                            pltpu.VMEM((rows, 1), F32), pltpu.VMEM((rows, kvw), F32)]),
        out_shape=jax.ShapeDtypeStruct((b, rows, kvw), F32),
        compiler_params=_cp("parallel", "arbitrary"), name="dsa_sample_attn",
    )(page_table, keys, q_bd, *([cache_k2] * pg), *([cache_v2] * pg), k_new, v_new)


MOE_TM = 512
GATHER_ROWS = 256


def _row_copy(src_ref, row, buf, r, sem):
    return pltpu.make_async_copy(src_ref.at[pl.ds(row, 1), :], buf.at[pl.ds(r, 1), :], sem)


def _gather_kernel(idx_ref, src_ref, o_ref, buf, sem, *, nrows):
    base = pl.program_id(0) * nrows

    def start(r, carry):
        _row_copy(src_ref, idx_ref[base + r], buf, r, sem).start()
        return carry

    def wait(r, carry):
        _row_copy(src_ref, idx_ref[base + r], buf, r, sem).wait()
        return carry

    lax.fori_loop(0, nrows, start, 0)
    lax.fori_loop(0, nrows, wait, 0)
    o_ref[...] = buf[...].astype(o_ref.dtype)


def _gather_rows(src, idx, out_dtype):
    p = idx.shape[0]
    d = src.shape[1]
    nrows = GATHER_ROWS
    return pl.pallas_call(
        functools.partial(_gather_kernel, nrows=nrows),
        grid_spec=pltpu.PrefetchScalarGridSpec(
            num_scalar_prefetch=1, grid=(p // nrows,),
            in_specs=[pl.BlockSpec(memory_space=pl.ANY)],
            out_specs=pl.BlockSpec((nrows, d), lambda i, idx: (i, 0)),
            scratch_shapes=[pltpu.VMEM((nrows, d), src.dtype), pltpu.SemaphoreType.DMA]),
        out_shape=jax.ShapeDtypeStruct((p, d), out_dtype),
        compiler_params=_cp("arbitrary"), name="moe_gather",
    )(idx, src)


def _moe_up_kernel(te_ref, nu_ref, a_ref, w1_ref, w3_ref, o_ref):
    used = pl.program_id(1) < nu_ref[0]

    @pl.when(used)
    def _():
        a = a_ref[...]
        g = jnp.dot(a, w1_ref[...].astype(BF16), preferred_element_type=F32)
        u = jnp.dot(a, w3_ref[...].astype(BF16), preferred_element_type=F32)
        o_ref[...] = (_silu(g) * u).astype(BF16)

    @pl.when(jnp.logical_not(used))
    def _():
        o_ref[...] = jnp.zeros(o_ref.shape, BF16)


def _moe_down_kernel(te_ref, nu_ref, a_ref, w_ref, o_ref):
    @pl.when(pl.program_id(1) == 0)
    def _():
        o_ref[...] = jnp.zeros(o_ref.shape, F32)

    @pl.when(pl.program_id(0) < nu_ref[0])
    def _():
        o_ref[...] += jnp.dot(a_ref[...], w_ref[...].astype(BF16), preferred_element_type=F32)


def _combine_kernel(d1_ref, d2_ref, y_ref, x_ref, route_ref, gate_ref, o_ref, buf1, buf2, sem, *, nrows):
    base = pl.program_id(0) * nrows

    def start(r, carry):
        _row_copy(y_ref, d1_ref[base + r], buf1, r, sem).start()
        _row_copy(y_ref, d2_ref[base + r], buf2, r, sem).start()
        return carry

    def wait(r, carry):
        _row_copy(y_ref, d1_ref[base + r], buf1, r, sem).wait()
        _row_copy(y_ref, d2_ref[base + r], buf2, r, sem).wait()
        return carry

    lax.fori_loop(0, nrows, start, 0)
    lax.fori_loop(0, nrows, wait, 0)
    route = route_ref[...]
    f = route[:, 2:3] * buf1[...] + route[:, 3:4] * buf2[...]
    o_ref[...] = x_ref[...] + gate_ref[...] * f


def _moe(h, route, x, gate, w1, w3, w2, jm):
    n, d = h.shape
    f = w1.shape[-1]
    tm = MOE_TM
    n_tiles = (2 * n + N_EXPERTS * (tm - 1)) // tm
    p_max = n_tiles * tm

    e_pair = jnp.concatenate([route[:, 0], route[:, 1]]).astype(I32)
    onehot = (e_pair[:, None] == jnp.arange(N_EXPERTS, dtype=I32)[None, :]).astype(I32)
    running = jnp.cumsum(onehot, axis=0)
    counts = running[-1]
    rank = jnp.sum(onehot * (running - 1), axis=1)
    tiles_per = (counts + tm - 1) // tm
    tile_end = jnp.cumsum(tiles_per)
    pad_start = (tile_end - tiles_per) * tm
    dest = (jnp.sum(onehot * pad_start[None, :], axis=1) + rank).astype(I32)
    src_token = jnp.zeros((p_max,), I32).at[dest].set(jnp.arange(2 * n, dtype=I32) % n)
    n_used = tile_end[-1:].astype(I32)
    tile_expert = jnp.minimum(
        jnp.sum((jnp.arange(n_tiles, dtype=I32)[:, None] >= tile_end[None, :]).astype(I32), axis=1),
        N_EXPERTS - 1).astype(I32)

    hs = _gather_rows(h, src_token, BF16)

    tf = 1024
    lim = lambda i, nu: jnp.minimum(i, nu[0] - 1)
    act = pl.pallas_call(
        _moe_up_kernel,
        grid_spec=pltpu.PrefetchScalarGridSpec(
            num_scalar_prefetch=2, grid=(f // tf, n_tiles),
            in_specs=[pl.BlockSpec((tm, d), lambda j, i, te, nu: (lim(i, nu), 0)),
                      pl.BlockSpec((None, None, d, tf), lambda j, i, te, nu: (jm, te[lim(i, nu)], 0, j)),
                      pl.BlockSpec((None, None, d, tf), lambda j, i, te, nu: (jm, te[lim(i, nu)], 0, j))],
            out_specs=pl.BlockSpec((tm, tf), lambda j, i, te, nu: (i, j))),
        out_shape=jax.ShapeDtypeStruct((p_max, f), BF16),
        compiler_params=_cp("arbitrary", "arbitrary"), name="moe_up",
    )(tile_expert, n_used, hs, w1, w3)

    tk = 1024
    nk = f // tk
    klim = lambda i, k, nu: jnp.where(i < nu[0], k, nk - 1)
    y = pl.pallas_call(
        _moe_down_kernel,
        grid_spec=pltpu.PrefetchScalarGridSpec(
            num_scalar_prefetch=2, grid=(n_tiles, nk),
            in_specs=[pl.BlockSpec((tm, tk), lambda i, k, te, nu: (lim(i, nu), klim(i, k, nu))),
                      pl.BlockSpec((None, None, tk, d),
                                   lambda i, k, te, nu: (jm, te[lim(i, nu)], klim(i, k, nu), 0))],
            out_specs=pl.BlockSpec((tm, d), lambda i, k, te, nu: (i, 0))),
        out_shape=jax.ShapeDtypeStruct((p_max, d), F32),
        compiler_params=_cp("arbitrary", "arbitrary"), name="moe_down",
    )(tile_expert, n_used, act, w2)

    nrows = min(GATHER_ROWS, n)
    if gate.ndim == 3:
        per = n // gate.shape[0] // nrows
        gate_spec = pl.BlockSpec((None, 1, d), lambda i, d1, d2: (i // per, 0, 0))
    else:
        gate_spec = pl.BlockSpec((nrows, d), lambda i, d1, d2: (i, 0))
    row_spec = pl.BlockSpec((nrows, d), lambda i, d1, d2: (i, 0))
    return pl.pallas_call(
        functools.partial(_combine_kernel, nrows=nrows),
        grid_spec=pltpu.PrefetchScalarGridSpec(
            num_scalar_prefetch=2, grid=(n // nrows,),
            in_specs=[pl.BlockSpec(memory_space=pl.ANY), row_spec,
                      pl.BlockSpec((nrows, LANES), lambda i, d1, d2: (i, 0)), gate_spec],
            out_specs=row_spec,
            scratch_shapes=[pltpu.VMEM((nrows, d), F32), pltpu.VMEM((nrows, d), F32),
                            pltpu.SemaphoreType.DMA]),
        out_shape=jax.ShapeDtypeStruct((n, d), F32),
        compiler_params=_cp("arbitrary"), name="moe_combine",
    )(dest[:n], dest[n:], y, x, route, gate)


def _rope_tables(pos):
    half = HEAD_DIM // 2
    inv = ROPE_THETA ** (-jnp.arange(half, dtype=F32) / half)
    ang = pos.astype(F32)[:, None] * inv[None, :]
    return jnp.cos(ang)[:, None, :], jnp.sin(ang)[:, None, :]


def _rope(x, cos, sin):
    half = x.shape[-1] // 2
    x1 = x[..., :half]
    x2 = x[..., half:]
    return jnp.concatenate([x1 * cos - x2 * sin, x2 * cos + x1 * sin], axis=-1)


def _heads_first(x, nh):
    b, t, _ = x.shape
    return x.reshape(b, t, nh, HEAD_DIM).transpose(0, 2, 1, 3)


def _heads_last(x):
    b, nh, t, d = x.shape
    return x.transpose(0, 2, 1, 3).reshape(b, t, nh * d)


def _pad_t(x, tp):
    t = x.shape[1]
    return x if t == tp else jnp.pad(x, ((0, 0), (0, tp - t), (0, 0)))


def _gate_arg(g, t, m):
    if t >= 256:
        return g[:, None, :]
    return jnp.repeat(g, t, axis=0)


def _trunk(x, mod, pos, ret_s0, rwkv_s0, shift0, W, sample=None):
    b, t, d = x.shape
    m = b * t
    cos, sin = _rope_tables(pos)
    ret_c = math.gcd(t, RET_CHUNK)
    ret_cp = RET_CHUNK
    t_ret = -(-t // ret_cp) * ret_cp
    t_rwkv = -(-t // RWKV_CHUNK) * RWKV_CHUNK
    rwkv_cols = 3 * RWKV_W + DECAY_LORA + AAA_LORA + GATE_LORA
    ret_cols = 4 * RET_W
    att_cols = ATT_W + 2 * ATT_KV_W + IDX_HEADS * IDX_DIM + IDX_DIM + IDX_HEADS
    per_layer = []
    depth = W["w_in"].shape[0]
    for l in range(depth):
        ml = mod[l]
        sh1, sc1, g1, sh2, sc2, g2 = [ml[:, i * d:(i + 1) * d] for i in range(6)]
        h = _norm_mod(x, W["norm_mix"][l], sh1, sc1, BF16)
        proj = _matmul(h.reshape(m, d), W["w_in"], (l,), 1024, 768).reshape(b, t, -1)

        rq = _rope(proj[..., 0:RET_W].reshape(b, t, RET_HEADS, HEAD_DIM), cos, sin)
        rk = _rope(proj[..., RET_W:2 * RET_W].reshape(b, t, RET_HEADS, HEAD_DIM), cos, sin) * HEAD_DIM ** -0.5
        rv = proj[..., 2 * RET_W:3 * RET_W]
        rg = proj[..., 3 * RET_W:4 * RET_W]
        hf = lambda z: _pad_t(z.reshape(b, t, RET_W), t_ret)
        ro, ret_state = _retention(hf(rq), hf(rk), hf(rv), hf(rg), ret_s0[l], W["ret_gn"][l], ret_c, ret_cp)
        ro = ro[:, :t]

        o = ret_cols
        q = _rope(proj[..., o:o + ATT_W].reshape(b, t, ATT_HEADS, HEAD_DIM), cos, sin)
        o += ATT_W
        k = _rope(proj[..., o:o + ATT_KV_W].reshape(b, t, ATT_KV_HEADS, HEAD_DIM), cos, sin)
        o += ATT_KV_W
        v = proj[..., o:o + ATT_KV_W].reshape(b, t, ATT_KV_HEADS, HEAD_DIM)
        o += ATT_KV_W
        qi = _rope(proj[..., o:o + IDX_HEADS * IDX_DIM].reshape(b, t, IDX_HEADS, IDX_DIM), cos, sin)
        o += IDX_HEADS * IDX_DIM
        ki = _rope(proj[..., o:o + IDX_DIM].reshape(b, t, 1, IDX_DIM), cos, sin)[:, :, 0]
        o += IDX_DIM
        wi = proj[..., o:o + IDX_HEADS]
        if sample is None:
            topk = min(TOPK_MAX, t // 4)
            ch = min(512, t)
            nc = t // ch
            kit = ki.astype(BF16).reshape(b, nc, ch, IDX_DIM).transpose(0, 1, 3, 2)
            kh = k.astype(BF16).transpose(0, 2, 1, 3).reshape(b, ATT_KV_HEADS, nc, ch, HEAD_DIM)
            vt = v.astype(BF16).transpose(0, 2, 1, 3)
            vh = jnp.concatenate([vt, jnp.ones(vt.shape[:-1] + (1,), BF16),
                                  jnp.zeros(vt.shape[:-1] + (LANES - HEAD_DIM - 1,), BF16)],
                                 axis=-1).reshape(b, ATT_KV_HEADS, nc, ch, LANES)
            ao = _dsa_prompt(qi.astype(BF16).transpose(0, 2, 1, 3), wi, kit,
                             (q * HEAD_DIM ** -0.5).astype(BF16).transpose(0, 2, 1, 3),
                             kh.transpose(0, 1, 2, 4, 3), vh, topk)
            ao = _heads_last(ao)
        else:
            cache_ik, cache_k2, cache_v2, page_table = sample
            past = page_table.shape[1] * PAGE_SIZE
            topk = min(TOPK_MAX, (past + t) // 4)
            rows = ATT_HEADS * t
            qi_rows = qi.astype(BF16).transpose(0, 2, 1, 3).reshape(b, rows, IDX_DIM)
            wcol = wi.transpose(0, 2, 1).reshape(b, rows, 1)
            padp = lambda z: jnp.pad(z, ((0, 0), (0, PAGE_SIZE - t), (0, 0)))
            grp = ATT_HEADS // ATT_KV_HEADS
            qh = (q * HEAD_DIM ** -0.5).astype(BF16).transpose(0, 2, 1, 3).reshape(
                b, ATT_KV_HEADS, grp * t, HEAD_DIM)
            q_bd = (qh[:, :, :, None, :] * jnp.eye(ATT_KV_HEADS, dtype=BF16)[None, :, None, :, None]
                    ).reshape(b, rows, ATT_KV_W)
            acc = _dsa_sample(l, qi_rows, wcol, padp(ki), q_bd, padp(k.reshape(b, t, ATT_KV_W)),
                              padp(v.reshape(b, t, ATT_KV_W)), cache_ik, cache_k2, cache_v2, page_table, topk)
            acc = acc.reshape(b, ATT_KV_HEADS, grp, t, ATT_KV_HEADS, HEAD_DIM)
            ao = jnp.stack([acc[:, g_, :, :, g_, :] for g_ in range(ATT_KV_HEADS)], axis=1)
            ao = ao.reshape(b, ATT_HEADS, t, HEAD_DIM).transpose(0, 2, 1, 3).reshape(b, t, ATT_W)

        p = proj[..., ret_cols + att_cols:]
        p_prev = jnp.concatenate([shift0[l][:, None, :], p[:, :-1]], axis=1)
        rr, kk_, vv_, lw, aa, gg = _rwkv_pre(p, p_prev, W["rwkv_mu"][l], W["rwkv_w0"][l], W["rwkv_w2"][l],
                                             W["rwkv_a0"][l], W["rwkv_a2"][l], W["rwkv_g2"][l])
        hp = lambda z: _pad_t(z, t_rwkv)
        wo, rwkv_state = _rwkv_scan(hp(rr), hp(kk_), hp(vv_), hp(lw), hp(aa), hp(gg), rwkv_s0[l],
                                    W["rwkv_kk"][l], W["rwkv_ka"][l], W["rwkv_rk"][l], W["rwkv_gn"][l],
                                    RWKV_CHUNK)
        wo = wo[:, :t]

        mixed = jnp.concatenate([ro, ao, wo], axis=-1).astype(BF16).reshape(m, d)
        x2 = _matmul(mixed, W["w_out"], (l,), 1024, 512, res=x.reshape(m, d), gate=_gate_arg(g1, t, m))

        j = l // 2
        g2a = _gate_arg(g2, t, m)
        if l % 2 == 0:
            h2 = _norm_mod(x2.reshape(b, t, d), W["norm_ffn"][l], sh2, sc2, BF16).reshape(m, d)
            act = _ffn_up(h2, W["ffn_w1"], W["ffn_w3"], (j,), 1024, 512)
            x3 = _matmul(act, W["ffn_w2"], (j,), 512, 512, res=x2, gate=g2a)
        else:
            h2, route = _norm_mod(x2.reshape(b, t, d), W["norm_ffn"][l], sh2, sc2, F32,
                                  router=(W["router_w"][j], W["router_b"][j]))
            x3 = _moe(h2.reshape(m, d), route.reshape(m, LANES), x2, g2a,
                      W["moe_w1"], W["moe_w3"], W["moe_w2"], j)
        x = x3.reshape(b, t, d)
        per_layer.append((k, v, ki, ret_state, rwkv_state, p[:, -1]))
    zeros = jnp.zeros((b, d), F32)
    y = _norm_mod(x, W["norm_final"], zeros, zeros, F32)
    stacked = [jnp.stack([st[i] for st in per_layer]) for i in range(6)]
    return y, stacked


def kernel(x_prompt, x_sample, c_prompt, c_sample, cache_k, cache_v, cache_idx_k, state_ret, state_rwkv,
           state_rwkv_shift, page_table, w_ada, b_ada, norm_mix, norm_ffn, w_in, w_out, ret_gn, rwkv_mu,
           rwkv_w0, rwkv_w2, rwkv_a0, rwkv_a2, rwkv_g2, rwkv_kk, rwkv_ka, rwkv_rk, rwkv_gn, ffn_w1, ffn_w3,
           ffn_w2, router_w, router_b, moe_w1, moe_w3, moe_w2, norm_final):
    W = dict(norm_mix=norm_mix, norm_ffn=norm_ffn, w_in=w_in, w_out=w_out, ret_gn=ret_gn, rwkv_mu=rwkv_mu,
             rwkv_w0=rwkv_w0, rwkv_w2=rwkv_w2, rwkv_a0=rwkv_a0, rwkv_a2=rwkv_a2, rwkv_g2=rwkv_g2,
             rwkv_kk=rwkv_kk, rwkv_ka=rwkv_ka, rwkv_rk=rwkv_rk, rwkv_gn=rwkv_gn, ffn_w1=ffn_w1,
             ffn_w3=ffn_w3, ffn_w2=ffn_w2, router_w=router_w, router_b=router_b, moe_w1=moe_w1,
             moe_w3=moe_w3, moe_w2=moe_w2, norm_final=norm_final)
    b_p, t_p, d = x_prompt.shape
    b_s, t_s, _ = x_sample.shape
    depth = w_in.shape[0]
    npool = cache_k.shape[1]
    past = page_table.shape[1] * PAGE_SIZE

    rows = -(-(b_p + b_s) // 8) * 8
    c_all = jnp.concatenate([c_prompt, c_sample, jnp.zeros((rows - b_p - b_s, d), F32)], axis=0)
    mod = _ada_mod(c_all, w_ada, b_ada)
    mod_p = mod[:, :b_p]
    mod_s = mod[:, b_p:b_p + b_s]

    ret0 = jnp.zeros((depth, b_p, RET_HEADS, HEAD_DIM, HEAD_DIM), F32)
    rwkv0 = jnp.zeros((depth, b_p, RWKV_HEADS, HEAD_DIM, HEAD_DIM), F32)
    shift0 = jnp.zeros((depth, b_p, state_rwkv_shift.shape[-1]), F32)
    y_p, st_p = _trunk(x_prompt, mod_p, jnp.arange(t_p), ret0, rwkv0, shift0, W)

    sample = (cache_idx_k, cache_k, cache_v, page_table)
    y_s, st_s = _trunk(x_sample, mod_s, past + jnp.arange(t_s), state_ret, state_rwkv, state_rwkv_shift, W,
                       sample=sample)
    return (y_p, y_s, *st_p, *st_s)
```

```python
import functools
import math

import jax
import jax.numpy as jnp
from jax import lax
from jax.experimental import pallas as pl
from jax.experimental.pallas import tpu as pltpu

F32 = jnp.float32
BF16 = jnp.bfloat16
I32 = jnp.int32

HEAD_DIM = 64
RET_HEADS = 8
ATT_HEADS = 16
ATT_KV_HEADS = 4
RWKV_HEADS = 8
IDX_HEADS = 16
IDX_DIM = 64
RET_W = RET_HEADS * HEAD_DIM
ATT_W = ATT_HEADS * HEAD_DIM
ATT_KV_W = ATT_KV_HEADS * HEAD_DIM
RWKV_W = RWKV_HEADS * HEAD_DIM
IDX_SCALE = IDX_DIM ** -0.5 * IDX_HEADS ** -0.5
TOPK_MAX = 256
Q_BLOCK = 128
RET_CHUNK = 128
RWKV_CHUNK = 64
ROPE_THETA = 10000.0
DECAY_LORA = 96
AAA_LORA = 96
GATE_LORA = 256
PAGE_SIZE = 128
N_EXPERTS = 8
EPS = 1e-6
GN_EPS = 1e-5
RWKV_GN_EPS = 64e-5

LANES = 128
VMEM_LIMIT_BYTES = 56 * 1024 * 1024

INT_MIN = -2 ** 31
NEG_INF_KEY = INT_MIN + 0x7FFFFF
MASKED_LOGIT = -1e30

_NN = (((1,), (0,)), ((), ()))
_NT = (((1,), (1,)), ((), ()))
_TN = (((0,), (0,)), ((), ()))
_BNN = (((2,), (1,)), ((0,), (0,)))
_BNT = (((2,), (2,)), ((0,), (0,)))


def _cp(*sem):
    return pltpu.CompilerParams(dimension_semantics=sem, vmem_limit_bytes=VMEM_LIMIT_BYTES)


def _dot1(a, b, dims=_NN):
    return lax.dot_general(a.astype(BF16), b.astype(BF16), dims, preferred_element_type=F32)


def _split2(a):
    hi = a.astype(BF16)
    lo = (a - hi.astype(F32)).astype(BF16)
    return hi, lo


def _dot3(a, b, dims=_NN):
    ah, al = _split2(a)
    bh, bl = _split2(b)
    d = lambda x, y: lax.dot_general(x, y, dims, preferred_element_type=F32)
    return d(ah, bh) + (d(al, bh) + d(ah, bl))


def _silu(x):
    return x * jax.nn.sigmoid(x)


def _split_heads(x, nh):
    return jnp.stack([x[:, h * HEAD_DIM:(h + 1) * HEAD_DIM] for h in range(nh)], axis=0)


def _merge_heads(x):
    return jnp.concatenate([x[h] for h in range(x.shape[0])], axis=-1)


def _sort_key(s):
    bits = lax.bitcast_convert_type(s, I32)
    return jnp.where(bits < 0, bits ^ 0x7FFFFFFF, bits)


def _ada_kernel(c_ref, w_ref, b_ref, o_ref):
    a = _silu(c_ref[...]).astype(BF16)
    o_ref[...] = jnp.dot(a, w_ref[...].astype(BF16), preferred_element_type=F32) + b_ref[...]


def _ada_mod(c_all, w_ada, b_ada):
    nl, d, n6 = w_ada.shape
    r = c_all.shape[0]
    tn = 1024
    return pl.pallas_call(
        _ada_kernel,
        grid=(nl, n6 // tn),
        in_specs=[pl.BlockSpec((r, d), lambda l, j: (0, 0)),
                  pl.BlockSpec((None, d, tn), lambda l, j: (l, 0, j)),
                  pl.BlockSpec((None, 1, tn), lambda l, j: (l, 0, j))],
        out_specs=pl.BlockSpec((None, r, tn), lambda l, j: (l, 0, j)),
        out_shape=jax.ShapeDtypeStruct((nl, r, n6), F32),
        compiler_params=_cp("parallel", "parallel"),
        name="ada_mod",
    )(c_all, w_ada, b_ada.reshape(nl, 1, n6))


def _norm_rows(x_ref, g_ref, sh_ref, sc_ref):
    x = x_ref[...]
    y = x * lax.rsqrt(jnp.mean(x * x, axis=-1, keepdims=True) + EPS)
    return y * g_ref[...] * (1.0 + sc_ref[...]) + sh_ref[...]


def _norm_mod_kernel(x_ref, g_ref, sh_ref, sc_ref, o_ref):
    o_ref[...] = _norm_rows(x_ref, g_ref, sh_ref, sc_ref).astype(o_ref.dtype)


def _norm_route_kernel(x_ref, g_ref, sh_ref, sc_ref, rw_ref, rb_ref, o_ref, route_ref):
    h = _norm_rows(x_ref, g_ref, sh_ref, sc_ref)
    o_ref[...] = h
    logits = _dot3(h, rw_ref[...]) + rb_ref[...]
    lane = lax.broadcasted_iota(I32, logits.shape, 1)
    logits = jnp.where(lane < N_EXPERTS, logits, -jnp.inf)
    m1 = jnp.max(logits, axis=-1, keepdims=True)
    i1 = jnp.min(jnp.where(logits == m1, lane, LANES), axis=-1, keepdims=True)
    rest = jnp.where(lane == i1, -jnp.inf, logits)
    m2 = jnp.max(rest, axis=-1, keepdims=True)
    i2 = jnp.min(jnp.where(rest == m2, lane, LANES), axis=-1, keepdims=True)
    e2 = jnp.exp(m2 - m1)
    g1 = 1.0 / (1.0 + e2)
    g2 = e2 / (1.0 + e2)
    route_ref[...] = jnp.where(lane == 0, i1.astype(F32),
                     jnp.where(lane == 1, i2.astype(F32),
                     jnp.where(lane == 2, g1, jnp.where(lane == 3, g2, 0.0))))


def _norm_mod(x, g, shift, scale, out_dtype, router=None):
    b, t, d = x.shape
    tt = min(t, 512)
    grid = (b, t // tt)
    row_spec = pl.BlockSpec((None, tt, d), lambda i, j: (i, j, 0))
    vec_spec = pl.BlockSpec((1, d), lambda i, j: (0, 0))
    mod_spec = pl.BlockSpec((None, 1, d), lambda i, j: (i, 0, 0))
    args = [x, g.reshape(1, d), shift.reshape(b, 1, d), scale.reshape(b, 1, d)]
    if router is None:
        return pl.pallas_call(
            _norm_mod_kernel, grid=grid,
            in_specs=[row_spec, vec_spec, mod_spec, mod_spec],
            out_specs=row_spec,
            out_shape=jax.ShapeDtypeStruct((b, t, d), out_dtype),
            compiler_params=_cp("parallel", "parallel"), name="norm_mod",
        )(*args)
    rw, rb = router
    rw_pad = jnp.pad(rw, ((0, 0), (0, LANES - N_EXPERTS)))
    rb_pad = jnp.pad(rb, (0, LANES - N_EXPERTS)).reshape(1, LANES)
    return pl.pallas_call(
        _norm_route_kernel, grid=grid,
        in_specs=[row_spec, vec_spec, mod_spec, mod_spec,
                  pl.BlockSpec((d, LANES), lambda i, j: (0, 0)),
                  pl.BlockSpec((1, LANES), lambda i, j: (0, 0))],
        out_specs=[row_spec, pl.BlockSpec((None, tt, LANES), lambda i, j: (i, j, 0))],
        out_shape=[jax.ShapeDtypeStruct((b, t, d), F32), jax.ShapeDtypeStruct((b, t, LANES), F32)],
        compiler_params=_cp("parallel", "parallel"), name="norm_route",
    )(*args, rw_pad, rb_pad)


def _mm_kernel(a_ref, w_ref, o_ref):
    o_ref[...] = jnp.dot(a_ref[...], w_ref[...].astype(BF16), preferred_element_type=F32)


def _mm_res_kernel(a_ref, w_ref, x_ref, g_ref, o_ref):
    acc = jnp.dot(a_ref[...], w_ref[...].astype(BF16), preferred_element_type=F32)
    o_ref[...] = x_ref[...] + g_ref[...] * acc


def _gate_spec(gate, m, tm, tn):
    if gate.ndim == 3:
        rows = m // gate.shape[0]
        assert rows % tm == 0
        return pl.BlockSpec((None, 1, tn), lambda j, i: (i // (rows // tm), 0, j))
    return pl.BlockSpec((tm, tn), lambda j, i: (i, j))


def _matmul(a, w, lead, tm, tn, res=None, gate=None):
    m, k = a.shape
    n = w.shape[-1]
    tm = min(tm, m)
    tn = min(tn, n)
    nlead = len(lead)
    grid = (pl.cdiv(n, tn), m // tm)
    a_spec = pl.BlockSpec((tm, k), lambda j, i: (i, 0))
    w_spec = pl.BlockSpec((None,) * nlead + (k, tn), lambda j, i: lead + (0, j))
    o_spec = pl.BlockSpec((tm, tn), lambda j, i: (i, j))
    out_shape = jax.ShapeDtypeStruct((m, n), F32)
    if res is None:
        return pl.pallas_call(_mm_kernel, grid=grid, in_specs=[a_spec, w_spec], out_specs=o_spec,
                              out_shape=out_shape, compiler_params=_cp("parallel", "parallel"),
                              name="matmul")(a, w)
    return pl.pallas_call(_mm_res_kernel, grid=grid,
                          in_specs=[a_spec, w_spec, o_spec, _gate_spec(gate, m, tm, tn)],
                          out_specs=o_spec, out_shape=out_shape,
                          compiler_params=_cp("parallel", "parallel"), name="matmul_res")(a, w, res, gate)


def _ffn_up_kernel(a_ref, w1_ref, w3_ref, o_ref):
    a = a_ref[...]
    g = jnp.dot(a, w1_ref[...].astype(BF16), preferred_element_type=F32)
    u = jnp.dot(a, w3_ref[...].astype(BF16), preferred_element_type=F32)
    o_ref[...] = (_silu(g) * u).astype(BF16)


def _ffn_up(a, w1, w3, lead, tm, tf):
    m, k = a.shape
    f = w1.shape[-1]
    tm = min(tm, m)
    nlead = len(lead)
    w_spec = pl.BlockSpec((None,) * nlead + (k, tf), lambda j, i: lead + (0, j))
    return pl.pallas_call(
        _ffn_up_kernel, grid=(pl.cdiv(f, tf), m // tm),
        in_specs=[pl.BlockSpec((tm, k), lambda j, i: (i, 0)), w_spec, w_spec],
        out_specs=pl.BlockSpec((tm, tf), lambda j, i: (i, j)),
        out_shape=jax.ShapeDtypeStruct((m, f), BF16),
        compiler_params=_cp("parallel", "parallel"), name="ffn_up",
    )(a, w1, w3)


def _ret_kernel(q_ref, k_ref, v_ref, g_ref, s0_ref, dm_ref, qd_ref, kd_ref, cd_ref, gn_ref,
                o_ref, so_ref, s_scr, *, nh):
    c = pl.program_id(1)

    @pl.when(c == 0)
    def _():
        s_scr[...] = s0_ref[...]

    q = _split_heads(q_ref[...], nh)
    k = _split_heads(k_ref[...], nh)
    v = _split_heads(v_ref[...], nh)
    s = s_scr[...]
    att = _dot3(q, k, _BNT) * dm_ref[...]
    out = _dot3(att, v, _BNN) + _dot3(q, s, _BNN) * qd_ref[...]
    kd = k * kd_ref[...]
    for h in range(nh):
        s_scr[h] = s[h] * cd_ref[h] + _dot3(kd[h], v[h], _TN)
    xc = out - jnp.mean(out, axis=-1, keepdims=True)
    y = xc * lax.rsqrt(jnp.mean(xc * xc, axis=-1, keepdims=True) + GN_EPS)
    o_ref[...] = _silu(g_ref[...]) * _merge_heads(y * gn_ref[...])

    @pl.when(c == pl.num_programs(1) - 1)
    def _():
        so_ref[...] = s_scr[...]


def _ret_tables(c_true, cp):
    hh = jnp.arange(RET_HEADS, dtype=F32)
    log_g = jnp.log(1.0 - 2.0 ** (-5.0 - hh))
    i = jnp.arange(cp, dtype=F32)
    rel = i[:, None] - i[None, :]
    dmask = jnp.where(rel >= 0, jnp.exp(log_g[:, None, None] * jnp.maximum(rel, 0.0)), 0.0)
    q_dec = jnp.exp(log_g[:, None] * (i[None, :] + 1.0))[..., None]
    k_dec = jnp.exp(log_g[:, None] * (c_true - 1.0 - i[None, :]))[..., None]
    c_dec = jnp.exp(log_g * c_true).reshape(RET_HEADS, 1, 1)
    return dmask, q_dec, k_dec, c_dec


def _retention(q, k, v, g, s0, gn, c_true, cp):
    b, tp, w = q.shape
    nh, d = RET_HEADS, HEAD_DIM
    dmask, q_dec, k_dec, c_dec = _ret_tables(c_true, cp)
    seq = pl.BlockSpec((None, cp, w), lambda i, c: (i, c, 0))
    st = pl.BlockSpec((None, nh, d, d), lambda i, c: (i, 0, 0, 0))
    full = lambda shape: pl.BlockSpec(shape, lambda i, c: (0,) * len(shape))
    return pl.pallas_call(
        functools.partial(_ret_kernel, nh=nh), grid=(b, tp // cp),
        in_specs=[seq, seq, seq, seq, st, full((nh, cp, cp)), full((nh, cp, 1)), full((nh, cp, 1)),
                  full((nh, 1, 1)), full((nh, 1, d))],
        out_specs=[seq, st],
        out_shape=[jax.ShapeDtypeStruct((b, tp, w), F32), jax.ShapeDtypeStruct((b, nh, d, d), F32)],
        scratch_shapes=[pltpu.VMEM((nh, d, d), F32)],
        compiler_params=_cp("parallel", "arbitrary"), name="retention",
    )(q, k, v, g, s0, dmask, q_dec, k_dec, c_dec, gn.reshape(nh, 1, d))


def _rwkv_pre_kernel(p_ref, pp_ref, mu_ref, w0_ref, w2_ref, a0_ref, a2_ref, g2_ref,
                     r_ref, k_ref, v_ref, lw_ref, a_ref, g_ref):
    p = p_ref[...]
    xs = p + mu_ref[...] * (pp_ref[...] - p)
    w = RWKV_W
    r_ref[...] = xs[:, :w]
    k_ref[...] = xs[:, w:2 * w]
    v_ref[...] = xs[:, 2 * w:3 * w]
    tail = xs[:, 3 * w:]
    z = -(w0_ref[...] + _dot3(jnp.tanh(tail), w2_ref[...]))
    softplus = jnp.maximum(z, 0.0) + jnp.log(1.0 + jnp.exp(-jnp.abs(z)))
    lw_ref[...] = -jnp.exp(-softplus - 0.5)
    a_ref[...] = jax.nn.sigmoid(a0_ref[...] + _dot3(tail, a2_ref[...]))
    g_ref[...] = _dot3(jax.nn.sigmoid(tail), g2_ref[...])


def _rwkv_pre(p, p_prev, mu, w0, w2, a0, a2, g2):
    b, t, cols = p.shape
    w = RWKV_W
    tail = cols - 3 * w
    w2p = jnp.pad(w2, ((0, tail - DECAY_LORA), (0, 0)))
    a2p = jnp.pad(a2, ((DECAY_LORA, tail - DECAY_LORA - AAA_LORA), (0, 0)))
    g2p = jnp.pad(g2, ((DECAY_LORA + AAA_LORA, 0), (0, 0)))
    tt = min(t, 512)
    row = pl.BlockSpec((None, tt, cols), lambda i, j: (i, j, 0))
    out = pl.BlockSpec((None, tt, w), lambda i, j: (i, j, 0))
    full = lambda shape: pl.BlockSpec(shape, lambda i, j: (0,) * len(shape))
    osh = jax.ShapeDtypeStruct((b, t, w), F32)
    return pl.pallas_call(
        _rwkv_pre_kernel, grid=(b, t // tt),
        in_specs=[row, row, full((1, cols)), full((1, w)), full((tail, w)), full((1, w)),
                  full((tail, w)), full((tail, w))],
        out_specs=[out] * 6, out_shape=[osh] * 6,
        compiler_params=_cp("parallel", "parallel"), name="rwkv_pre",
    )(p, p_prev, mu.reshape(1, cols), w0.reshape(1, w), w2p, a0.reshape(1, w), a2p, g2p)


def _rwkv_kernel(r_ref, k_ref, v_ref, lw_ref, a_ref, g_ref, s0_ref, kkw_ref, kaw_ref, rkw_ref, gn_ref,
                 o_ref, so_ref, s_scr, *, nh, cs):
    c = pl.program_id(1)

    @pl.when(c == 0)
    def _():
        s_scr[...] = s0_ref[...]

    row = lax.broadcasted_iota(I32, (cs, cs), 0)
    col = lax.broadcasted_iota(I32, (cs, cs), 1)
    incl = row >= col
    strict = row > col
    ones_tril = jnp.broadcast_to(jnp.where(incl, 1.0, 0.0).astype(BF16)[None], (nh, cs, cs))
    eye = jnp.where(row == col, 1.0, 0.0)

    r = _split_heads(r_ref[...], nh)
    k = _split_heads(k_ref[...], nh)
    v = _split_heads(v_ref[...], nh)
    lw = _split_heads(lw_ref[...], nh)
    a = _split_heads(a_ref[...], nh)
    kk = k * kkw_ref[...]
    kk = kk * lax.rsqrt(jnp.sum(kk * kk, axis=-1, keepdims=True) + 1e-12)
    kmod = k * (1.0 + (a - 1.0) * kaw_ref[...])
    b = kk * a
    l1 = lw.astype(BF16)
    rem = lw - l1.astype(F32)
    l2 = rem.astype(BF16)
    l3 = (rem - l2.astype(F32)).astype(BF16)
    dd = lambda x: lax.dot_general(ones_tril, x, _BNN, preferred_element_type=F32)
    cum = dd(l1) + (dd(l2) + dd(l3))
    cum_end = cum[:, cs - 1:cs, :]
    e_inv = jnp.exp(-cum)
    e_rem = jnp.exp(cum_end - cum)
    a_t = -kk * jnp.exp(cum - lw)
    r_t = r * jnp.exp(cum)
    ar = jnp.concatenate([a_t, r_t], axis=1)
    mb = _dot3(ar, b * e_inv, _BNT)
    mk = _dot3(ar, kmod * e_inv, _BNT)
    l_ab = jnp.where(strict, mb[:, :cs], 0.0)
    p_rb = jnp.where(incl, mb[:, cs:], 0.0)
    l_ak = jnp.where(strict, mk[:, :cs], 0.0)
    p_rk = jnp.where(incl, mk[:, cs:], 0.0)
    pw = l_ab
    tinv = eye + l_ab
    for _ in range(int(math.log2(cs)) - 1):
        pw = _dot3(pw, pw, _BNN)
        tinv = tinv + _dot3(tinv, pw, _BNN)
    s0 = s_scr[...]
    hs = _dot3(ar, s0, _BNT)
    u = _dot3(tinv, hs[:, :cs] + _dot3(l_ak, v, _BNN), _BNN)
    y = hs[:, cs:] + _dot3(p_rb, u, _BNN) + _dot3(p_rk, v, _BNN)
    uv = jnp.concatenate([u, v], axis=1)
    bk = jnp.concatenate([b * e_rem, kmod * e_rem], axis=1)
    e_end = jnp.exp(cum_end)
    for h in range(nh):
        s_scr[h] = s0[h] * e_end[h] + _dot3(uv[h], bk[h], _TN)
    xc = y - jnp.mean(y, axis=-1, keepdims=True)
    yn = xc * lax.rsqrt(jnp.mean(xc * xc, axis=-1, keepdims=True) + RWKV_GN_EPS)
    bonus = jnp.sum(r * kmod * rkw_ref[...], axis=-1, keepdims=True) * v
    o_ref[...] = _merge_heads(yn * gn_ref[...] + bonus) * g_ref[...]

    @pl.when(c == pl.num_programs(1) - 1)
    def _():
        so_ref[...] = s_scr[...]


def _rwkv_scan(r, k, v, lw, a, g, s0, kkw, kaw, rkw, gn, cs):
    b, tp, w = r.shape
    nh, d = RWKV_HEADS, HEAD_DIM
    seq = pl.BlockSpec((None, cs, w), lambda i, c: (i, c, 0))
    st = pl.BlockSpec((None, nh, d, d), lambda i, c: (i, 0, 0, 0))
    par = pl.BlockSpec((nh, 1, d), lambda i, c: (0, 0, 0))
    return pl.pallas_call(
        functools.partial(_rwkv_kernel, nh=nh, cs=cs), grid=(b, tp // cs),
        in_specs=[seq] * 6 + [st] + [par] * 4,
        out_specs=[seq, st],
        out_shape=[jax.ShapeDtypeStruct((b, tp, w), F32), jax.ShapeDtypeStruct((b, nh, d, d), F32)],
        scratch_shapes=[pltpu.VMEM((nh, d, d), F32)],
        compiler_params=_cp("parallel", "arbitrary"), name="rwkv_scan",
    )(r, k, v, lw, a, g, s0, kkw.reshape(nh, 1, d), kaw.reshape(nh, 1, d), rkw.reshape(nh, 1, d),
      gn.reshape(nh, 1, d))


def _kth_largest(count_ge, shape, topk):
    def bit_body(it, res):
        cand = res | lax.shift_left(jnp.int32(1), 31 - it)
        cnt = count_ge(cand ^ INT_MIN)
        return jnp.where(cnt >= topk, cand, res)

    res = lax.fori_loop(0, 32, bit_body, jnp.zeros(shape, I32))
    return res ^ INT_MIN


def _dsa_prompt_kernel(qi_ref, wi_ref, kit_ref, q_ref, kt_ref, v_ref, o_ref,
                       key_scr, bias_scr, lg_scr, m_scr, acc_scr, *, ch, topk):
    i = pl.program_id(1)
    nq = Q_BLOCK
    nch = (i * nq + nq + ch - 1) // ch
    qpos = i * nq + lax.broadcasted_iota(I32, (nq, ch), 0)
    lane = lax.broadcasted_iota(I32, (nq, ch), 1)
    wi = wi_ref[...]

    def score_body(j, carry):
        kit = kit_ref[j]
        s = jnp.zeros((nq, ch), F32)
        for h in range(IDX_HEADS):
            rel = jnp.dot(qi_ref[h], kit, preferred_element_type=F32)
            s = s + wi[:, h:h + 1] * jnp.maximum(rel, 0.0)
        s = s * IDX_SCALE
        s = jnp.where(s == 0.0, 0.0, s)
        s = jnp.where(j * ch + lane <= qpos, s, -jnp.inf)
        key_scr[j] = _sort_key(s)
        return carry

    lax.fori_loop(0, nch, score_body, 0)

    def count_ge(t):
        def body(j, acc):
            m = jnp.where(key_scr[j] >= t, 1.0, 0.0)
            part = m[:, 0:LANES]
            for u in range(1, ch // LANES):
                part = part + m[:, u * LANES:(u + 1) * LANES]
            return acc + part

        acc = lax.fori_loop(0, nch, body, jnp.zeros((nq, LANES), F32))
        return jnp.sum(acc, axis=1, keepdims=True)

    thr = _kth_largest(count_ge, (nq, 1), float(topk))
    thr = jnp.maximum(thr, NEG_INF_KEY + 1)

    def bias_body(j, carry):
        bias_scr[j] = jnp.where(key_scr[j] >= thr, 0.0, MASKED_LOGIT)
        return carry

    lax.fori_loop(0, nch, bias_body, 0)

    hpg = ATT_HEADS // ATT_KV_HEADS
    ntile = ch // LANES
    for g in range(ATT_KV_HEADS):
        m_scr[...] = jnp.full(m_scr.shape, MASKED_LOGIT, F32)
        acc_scr[...] = jnp.zeros(acc_scr.shape, F32)

        def logit_body(j, carry, g=g):
            kt = kt_ref[g, j]
            bias = bias_scr[j]
            for hh in range(hpg):
                lg = jnp.dot(q_ref[g * hpg + hh], kt, preferred_element_type=F32) + bias
                lg_scr[hh, j] = lg
                part = lg[:, 0:LANES]
                for u in range(1, ntile):
                    part = jnp.maximum(part, lg[:, u * LANES:(u + 1) * LANES])
                m_scr[hh] = jnp.maximum(m_scr[hh], part)
            return carry

        lax.fori_loop(0, nch, logit_body, 0)
        for hh in range(hpg):
            m_scr[hh] = jnp.broadcast_to(jnp.max(m_scr[hh], axis=-1, keepdims=True), (nq, LANES))

        def pv_body(j, carry, g=g):
            vv = v_ref[g, j]
            for hh in range(hpg):
                m = m_scr[hh]
                lg = lg_scr[hh, j]
                p = jnp.concatenate(
                    [jnp.exp(lg[:, u * LANES:(u + 1) * LANES] - m).astype(BF16) for u in range(ntile)], axis=1)
                acc_scr[hh] += jnp.dot(p, vv, preferred_element_type=F32)
            return carry

        lax.fori_loop(0, nch, pv_body, 0)
        for hh in range(hpg):
            acc = acc_scr[hh]
            o_ref[g * hpg + hh] = acc[:, :HEAD_DIM] / acc[:, HEAD_DIM:HEAD_DIM + 1]


def _dsa_prompt(qi_h, wi, kit, q_h, kt, v_c, topk):
    b, nh, t, d = q_h.shape
    nc, ch = kit.shape[1], kit.shape[3]
    hpg = ATT_HEADS // ATT_KV_HEADS
    qspec = pl.BlockSpec((None, nh, Q_BLOCK, d), lambda i, j: (i, 0, j, 0))
    return pl.pallas_call(
        functools.partial(_dsa_prompt_kernel, ch=ch, topk=topk), grid=(b, t // Q_BLOCK),
        in_specs=[qspec,
                  pl.BlockSpec((None, Q_BLOCK, IDX_HEADS), lambda i, j: (i, j, 0)),
                  pl.BlockSpec((None, nc, IDX_DIM, ch), lambda i, j: (i, 0, 0, 0)),
                  qspec,
                  pl.BlockSpec((None, ATT_KV_HEADS, nc, d, ch), lambda i, j: (i, 0, 0, 0, 0)),
                  pl.BlockSpec((None, ATT_KV_HEADS, nc, ch, LANES), lambda i, j: (i, 0, 0, 0, 0))],
        out_specs=qspec,
        out_shape=jax.ShapeDtypeStruct((b, nh, t, d), F32),
        scratch_shapes=[pltpu.VMEM((nc, Q_BLOCK, ch), I32), pltpu.VMEM((nc, Q_BLOCK, ch), F32),
                        pltpu.VMEM((hpg, nc, Q_BLOCK, ch), F32), pltpu.VMEM((hpg, Q_BLOCK, LANES), F32),
                        pltpu.VMEM((hpg, Q_BLOCK, LANES), F32)],
        compiler_params=_cp("parallel", "arbitrary"), name="dsa_prompt",
    )(qi_h, wi, kit, q_h, kt, v_c)


def _dsa_s_score_kernel(pt_ref, qi_ref, w_ref, *rest, pg, ns, tq, past):
    kc_refs = rest[:pg]
    kn_ref = rest[pg]
    o_ref = rest[pg + 1]
    j = pl.program_id(1)
    qi = qi_ref[...]
    wcol = w_ref[...]
    qpos = past + lax.broadcasted_iota(I32, (tq, PAGE_SIZE), 0)
    lane = lax.broadcasted_iota(I32, (tq, PAGE_SIZE), 1)

    def page_keys(kpage, kbase):
        rel = jnp.maximum(_dot1(qi, kpage, _NT), 0.0) * wcol
        s = jnp.sum(rel.reshape(IDX_HEADS, tq, PAGE_SIZE), axis=0) * IDX_SCALE
        s = jnp.where(s == 0.0, 0.0, s)
        s = jnp.where(kbase + lane <= qpos, s, -jnp.inf)
        return _sort_key(s)

    @pl.when(j < ns)
    def _():
        for u in range(pg):
            o_ref[u] = page_keys(kc_refs[u][...], (j * pg + u) * PAGE_SIZE)

    @pl.when(j == ns)
    def _():
        o_ref[0] = page_keys(kn_ref[...], past)
        for u in range(1, pg):
            o_ref[u] = jnp.full((tq, PAGE_SIZE), NEG_INF_KEY, I32)


def _dsa_s_attn_kernel(pt_ref, keys_ref, q_ref, *rest, pg, ns, tq, topk):
    kc_refs = rest[:pg]
    vc_refs = rest[pg:2 * pg]
    kn_ref, vn_ref, o_ref, thr_scr, m_scr, l_scr, acc_scr = rest[2 * pg:]
    j = pl.program_id(1)

    @pl.when(j == 0)
    def _():
        def count_ge(t):
            m = jnp.where(keys_ref[...] >= t[None], 1.0, 0.0)
            return jnp.sum(jnp.sum(m, axis=0), axis=1, keepdims=True)

        thr = _kth_largest(count_ge, (tq, 1), float(topk))
        thr_scr[...] = jnp.broadcast_to(jnp.maximum(thr, NEG_INF_KEY + 1), thr_scr.shape)
        m_scr[...] = jnp.full(m_scr.shape, MASKED_LOGIT, F32)
        l_scr[...] = jnp.zeros(l_scr.shape, F32)
        acc_scr[...] = jnp.zeros(acc_scr.shape, F32)

    q = q_ref[...]
    thr = thr_scr[...]

    def pages(slots, kps, vps):
        lgs = []
        for slot, kp in zip(slots, kps):
            bias8 = jnp.where(keys_ref[slot] >= thr, 0.0, MASKED_LOGIT)
            bias = jnp.broadcast_to(bias8[None], (ATT_HEADS, tq, PAGE_SIZE)).reshape(ATT_HEADS * tq, PAGE_SIZE)
            lgs.append(_dot1(q, kp, _NT) + bias)
        lg = lgs[0] if len(lgs) == 1 else jnp.concatenate(lgs, axis=1)
        m_old = m_scr[...]
        m_new = jnp.maximum(m_old, jnp.max(lg, axis=-1, keepdims=True))
        alpha = jnp.exp(m_old - m_new)
        p = jnp.exp(lg - m_new)
        l_scr[...] = alpha * l_scr[...] + jnp.sum(p, axis=-1, keepdims=True)
        acc = alpha * acc_scr[...]
        for n, vp in enumerate(vps):
            acc = acc + _dot1(p[:, n * PAGE_SIZE:(n + 1) * PAGE_SIZE], vp)
        acc_scr[...] = acc
        m_scr[...] = m_new

    def flat(ref):
        return jnp.concatenate([ref[pl.ds(g, PAGE_SIZE, stride=ATT_KV_HEADS), :] for g in range(ATT_KV_HEADS)],
                               axis=-1)

    @pl.when(j < ns)
    def _():
        pages([j * pg + u for u in range(pg)], [flat(r) for r in kc_refs], [flat(r) for r in vc_refs])

    @pl.when(j == ns)
    def _():
        pages([ns * pg], [kn_ref[...]], [vn_ref[...]])
        o_ref[...] = acc_scr[...] / l_scr[...]


def _dsa_sample(l, qi_rows, wcol, ki_new, q_bd, k_new, v_new, cache_ik, cache_k2, cache_v2, page_table, topk):
    b, rows, _ = qi_rows.shape
    tq = rows // IDX_HEADS
    npages = page_table.shape[1]
    pg = 8
    ns = npages // pg
    past = npages * PAGE_SIZE
    kvw = ATT_KV_W

    def cache_spec(u, width):
        return pl.BlockSpec((None, None, PAGE_SIZE, width),
                            lambda i, j, pt: (l, pt[i, jnp.minimum(j, ns - 1) * pg + u], 0, 0))

    def kv_spec(u):
        return pl.BlockSpec((None, None, PAGE_SIZE * ATT_KV_HEADS, HEAD_DIM),
                            lambda i, j, pt: (l, pt[i, jnp.minimum(j, ns - 1) * pg + u], 0, 0))

    per_b = lambda shape: pl.BlockSpec((None,) + shape, lambda i, j, pt: (i,) + (0,) * len(shape))
    keys = pl.pallas_call(
        functools.partial(_dsa_s_score_kernel, pg=pg, ns=ns, tq=tq, past=past),
        grid_spec=pltpu.PrefetchScalarGridSpec(
            num_scalar_prefetch=1, grid=(b, ns + 1),
            in_specs=[per_b((rows, IDX_DIM)), per_b((rows, 1))]
                     + [cache_spec(u, IDX_DIM) for u in range(pg)] + [per_b((PAGE_SIZE, IDX_DIM))],
            out_specs=pl.BlockSpec((None, pg, tq, PAGE_SIZE), lambda i, j, pt: (i, j, 0, 0))),
        out_shape=jax.ShapeDtypeStruct((b, (ns + 1) * pg, tq, PAGE_SIZE), I32),
        compiler_params=_cp("parallel", "arbitrary"), name="dsa_sample_score",
    )(page_table, qi_rows, wcol, *([cache_ik] * pg), ki_new)

    nslots = (ns + 1) * pg
    return pl.pallas_call(
        functools.partial(_dsa_s_attn_kernel, pg=pg, ns=ns, tq=tq, topk=topk),
        grid_spec=pltpu.PrefetchScalarGridSpec(
            num_scalar_prefetch=1, grid=(b, ns + 1),
            in_specs=[per_b((nslots, tq, PAGE_SIZE)), per_b((rows, kvw))]
                     + [kv_spec(u) for u in range(pg)] + [kv_spec(u) for u in range(pg)]
                     + [per_b((PAGE_SIZE, kvw)), per_b((PAGE_SIZE, kvw))],
            out_specs=per_b((rows, kvw)),
            scratch_shapes=[pltpu.VMEM((tq, PAGE_SIZE), I32), pltpu.VMEM((rows, 1), F32),
                            pltpu.VMEM((rows, 1), F32), pltpu.VMEM((rows, kvw), F32)]),
        out_shape=jax.ShapeDtypeStruct((b, rows, kvw), F32),
        compiler_params=_cp("parallel", "arbitrary"), name="dsa_sample_attn",
    )(page_table, keys, q_bd, *([cache_k2] * pg), *([cache_v2] * pg), k_new, v_new)


MOE_TM = 512
GATHER_ROWS = 256


def _row_copy(src_ref, row, buf, r, sem):
    return pltpu.make_async_copy(src_ref.at[pl.ds(row, 1), :], buf.at[pl.ds(r, 1), :], sem)


def _gather_kernel(idx_ref, src_ref, o_ref, buf, sem, *, nrows):
    base = pl.program_id(0) * nrows

    def start(r, carry):
        _row_copy(src_ref, idx_ref[base + r], buf, r, sem).start()
        return carry

    def wait(r, carry):
        _row_copy(src_ref, idx_ref[base + r], buf, r, sem).wait()
        return carry

    lax.fori_loop(0, nrows, start, 0)
    lax.fori_loop(0, nrows, wait, 0)
    o_ref[...] = buf[...].astype(o_ref.dtype)


def _gather_rows(src, idx, out_dtype):
    p = idx.shape[0]
    d = src.shape[1]
    nrows = GATHER_ROWS
    return pl.pallas_call(
        functools.partial(_gather_kernel, nrows=nrows),
        grid_spec=pltpu.PrefetchScalarGridSpec(
            num_scalar_prefetch=1, grid=(p // nrows,),
            in_specs=[pl.BlockSpec(memory_space=pl.ANY)],
            out_specs=pl.BlockSpec((nrows, d), lambda i, idx: (i, 0)),
            scratch_shapes=[pltpu.VMEM((nrows, d), src.dtype), pltpu.SemaphoreType.DMA]),
        out_shape=jax.ShapeDtypeStruct((p, d), out_dtype),
        compiler_params=_cp("arbitrary"), name="moe_gather",
    )(idx, src)


def _moe_up_kernel(te_ref, nu_ref, a_ref, w1_ref, w3_ref, o_ref):
    used = pl.program_id(1) < nu_ref[0]

    @pl.when(used)
    def _():
        a = a_ref[...]
        g = jnp.dot(a, w1_ref[...].astype(BF16), preferred_element_type=F32)
        u = jnp.dot(a, w3_ref[...].astype(BF16), preferred_element_type=F32)
        o_ref[...] = (_silu(g) * u).astype(BF16)

    @pl.when(jnp.logical_not(used))
    def _():
        o_ref[...] = jnp.zeros(o_ref.shape, BF16)


def _moe_down_kernel(te_ref, nu_ref, a_ref, w_ref, o_ref):
    @pl.when(pl.program_id(1) == 0)
    def _():
        o_ref[...] = jnp.zeros(o_ref.shape, F32)

    @pl.when(pl.program_id(0) < nu_ref[0])
    def _():
        o_ref[...] += jnp.dot(a_ref[...], w_ref[...].astype(BF16), preferred_element_type=F32)


def _combine_kernel(d1_ref, d2_ref, y_ref, x_ref, route_ref, gate_ref, o_ref, buf1, buf2, sem, *, nrows):
    base = pl.program_id(0) * nrows

    def start(r, carry):
        _row_copy(y_ref, d1_ref[base + r], buf1, r, sem).start()
        _row_copy(y_ref, d2_ref[base + r], buf2, r, sem).start()
        return carry

    def wait(r, carry):
        _row_copy(y_ref, d1_ref[base + r], buf1, r, sem).wait()
        _row_copy(y_ref, d2_ref[base + r], buf2, r, sem).wait()
        return carry

    lax.fori_loop(0, nrows, start, 0)
    lax.fori_loop(0, nrows, wait, 0)
    route = route_ref[...]
    f = route[:, 2:3] * buf1[...] + route[:, 3:4] * buf2[...]
    o_ref[...] = x_ref[...] + gate_ref[...] * f


def _moe(h, route, x, gate, w1, w3, w2, jm):
    n, d = h.shape
    f = w1.shape[-1]
    tm = MOE_TM
    n_tiles = (2 * n + N_EXPERTS * (tm - 1)) // tm
    p_max = n_tiles * tm

    e_pair = jnp.concatenate([route[:, 0], route[:, 1]]).astype(I32)
    onehot = (e_pair[:, None] == jnp.arange(N_EXPERTS, dtype=I32)[None, :]).astype(I32)
    running = jnp.cumsum(onehot, axis=0)
    counts = running[-1]
    rank = jnp.sum(onehot * (running - 1), axis=1)
    tiles_per = (counts + tm - 1) // tm
    tile_end = jnp.cumsum(tiles_per)
    pad_start = (tile_end - tiles_per) * tm
    dest = (jnp.sum(onehot * pad_start[None, :], axis=1) + rank).astype(I32)
    dest = jnp.where(e_pair < N_EXPERTS, dest, p_max)
    src_token = jnp.zeros((p_max,), I32).at[dest].set(jnp.arange(2 * n, dtype=I32) % n, mode="drop")
    dest = jnp.minimum(dest, p_max - 1)
    n_used = tile_end[-1:].astype(I32)
    tile_expert = jnp.minimum(
        jnp.sum((jnp.arange(n_tiles, dtype=I32)[:, None] >= tile_end[None, :]).astype(I32), axis=1),
        N_EXPERTS - 1).astype(I32)

    hs = _gather_rows(h, src_token, BF16)

    tf = 1024
    lim = lambda i, nu: jnp.minimum(i, nu[0] - 1)
    act = pl.pallas_call(
        _moe_up_kernel,
        grid_spec=pltpu.PrefetchScalarGridSpec(
            num_scalar_prefetch=2, grid=(f // tf, n_tiles),
            in_specs=[pl.BlockSpec((tm, d), lambda j, i, te, nu: (lim(i, nu), 0)),
                      pl.BlockSpec((None, None, d, tf), lambda j, i, te, nu: (jm, te[lim(i, nu)], 0, j)),
                      pl.BlockSpec((None, None, d, tf), lambda j, i, te, nu: (jm, te[lim(i, nu)], 0, j))],
            out_specs=pl.BlockSpec((tm, tf), lambda j, i, te, nu: (i, j))),
        out_shape=jax.ShapeDtypeStruct((p_max, f), BF16),
        compiler_params=_cp("arbitrary", "arbitrary"), name="moe_up",
    )(tile_expert, n_used, hs, w1, w3)

    tk = 1024
    nk = f // tk
    klim = lambda i, k, nu: jnp.where(i < nu[0], k, nk - 1)
    y = pl.pallas_call(
        _moe_down_kernel,
        grid_spec=pltpu.PrefetchScalarGridSpec(
            num_scalar_prefetch=2, grid=(n_tiles, nk),
            in_specs=[pl.BlockSpec((tm, tk), lambda i, k, te, nu: (lim(i, nu), klim(i, k, nu))),
                      pl.BlockSpec((None, None, tk, d),
                                   lambda i, k, te, nu: (jm, te[lim(i, nu)], klim(i, k, nu), 0))],
            out_specs=pl.BlockSpec((tm, d), lambda i, k, te, nu: (i, 0))),
        out_shape=jax.ShapeDtypeStruct((p_max, d), F32),
        compiler_params=_cp("arbitrary", "arbitrary"), name="moe_down",
    )(tile_expert, n_used, act, w2)

    nrows = min(GATHER_ROWS, n)
    if gate.ndim == 3:
        per = n // gate.shape[0] // nrows
        gate_spec = pl.BlockSpec((None, 1, d), lambda i, d1, d2: (i // per, 0, 0))
    else:
        gate_spec = pl.BlockSpec((nrows, d), lambda i, d1, d2: (i, 0))
    row_spec = pl.BlockSpec((nrows, d), lambda i, d1, d2: (i, 0))
    return pl.pallas_call(
        functools.partial(_combine_kernel, nrows=nrows),
        grid_spec=pltpu.PrefetchScalarGridSpec(
            num_scalar_prefetch=2, grid=(n // nrows,),
            in_specs=[pl.BlockSpec(memory_space=pl.ANY), row_spec,
                      pl.BlockSpec((nrows, LANES), lambda i, d1, d2: (i, 0)), gate_spec],
            out_specs=row_spec,
            scratch_shapes=[pltpu.VMEM((nrows, d), F32), pltpu.VMEM((nrows, d), F32),
                            pltpu.SemaphoreType.DMA]),
        out_shape=jax.ShapeDtypeStruct((n, d), F32),
        compiler_params=_cp("arbitrary"), name="moe_combine",
    )(dest[:n], dest[n:], y, x, route, gate)


def _rope_tables(pos):
    half = HEAD_DIM // 2
    inv = ROPE_THETA ** (-jnp.arange(half, dtype=F32) / half)
    ang = pos.astype(F32)[:, None] * inv[None, :]
    return jnp.cos(ang)[:, None, :], jnp.sin(ang)[:, None, :]


def _rope(x, cos, sin):
    half = x.shape[-1] // 2
    x1 = x[..., :half]
    x2 = x[..., half:]
    return jnp.concatenate([x1 * cos - x2 * sin, x2 * cos + x1 * sin], axis=-1)


def _heads_first(x, nh):
    b, t, _ = x.shape
    return x.reshape(b, t, nh, HEAD_DIM).transpose(0, 2, 1, 3)


def _heads_last(x):
    b, nh, t, d = x.shape
    return x.transpose(0, 2, 1, 3).reshape(b, t, nh * d)


def _pad_t(x, tp):
    t = x.shape[1]
    return x if t == tp else jnp.pad(x, ((0, 0), (0, tp - t), (0, 0)))


def _gate_arg(g, t, m):
    if t >= 256:
        return g[:, None, :]
    return jnp.repeat(g, t, axis=0)


def _mix_layer(x, l, ml, pos, ret_s0, rwkv_s0, shift0, W, sample=None):
    b, t, d = x.shape
    m = b * t
    cos, sin = _rope_tables(pos)
    ret_c = math.gcd(t, RET_CHUNK)
    ret_cp = RET_CHUNK
    t_ret = -(-t // ret_cp) * ret_cp
    t_rwkv = -(-t // RWKV_CHUNK) * RWKV_CHUNK
    ret_cols = 4 * RET_W
    att_cols = ATT_W + 2 * ATT_KV_W + IDX_HEADS * IDX_DIM + IDX_DIM + IDX_HEADS
    sh1, sc1, g1 = [ml[:, i * d:(i + 1) * d] for i in range(3)]
    h = _norm_mod(x, W["norm_mix"][l], sh1, sc1, BF16)
    proj = _matmul(h.reshape(m, d), W["w_in"], (l,), 1024, 768).reshape(b, t, -1)

    rq = _rope(proj[..., 0:RET_W].reshape(b, t, RET_HEADS, HEAD_DIM), cos, sin)
    rk = _rope(proj[..., RET_W:2 * RET_W].reshape(b, t, RET_HEADS, HEAD_DIM), cos, sin) * HEAD_DIM ** -0.5
    rv = proj[..., 2 * RET_W:3 * RET_W]
    rg = proj[..., 3 * RET_W:4 * RET_W]
    hf = lambda z: _pad_t(z.reshape(b, t, RET_W), t_ret)
    ro, ret_state = _retention(hf(rq), hf(rk), hf(rv), hf(rg), ret_s0[l], W["ret_gn"][l], ret_c, ret_cp)
    ro = ro[:, :t]

    o = ret_cols
    q = _rope(proj[..., o:o + ATT_W].reshape(b, t, ATT_HEADS, HEAD_DIM), cos, sin)
    o += ATT_W
    k = _rope(proj[..., o:o + ATT_KV_W].reshape(b, t, ATT_KV_HEADS, HEAD_DIM), cos, sin)
    o += ATT_KV_W
    v = proj[..., o:o + ATT_KV_W].reshape(b, t, ATT_KV_HEADS, HEAD_DIM)
    o += ATT_KV_W
    qi = _rope(proj[..., o:o + IDX_HEADS * IDX_DIM].reshape(b, t, IDX_HEADS, IDX_DIM), cos, sin)
    o += IDX_HEADS * IDX_DIM
    ki = _rope(proj[..., o:o + IDX_DIM].reshape(b, t, 1, IDX_DIM), cos, sin)[:, :, 0]
    o += IDX_DIM
    wi = proj[..., o:o + IDX_HEADS]
    if sample is None:
        topk = min(TOPK_MAX, t // 4)
        ch = min(512, t)
        nc = t // ch
        kit = ki.astype(BF16).reshape(b, nc, ch, IDX_DIM).transpose(0, 1, 3, 2)
        kh = k.astype(BF16).transpose(0, 2, 1, 3).reshape(b, ATT_KV_HEADS, nc, ch, HEAD_DIM)
        vt = v.astype(BF16).transpose(0, 2, 1, 3)
        vh = jnp.concatenate([vt, jnp.ones(vt.shape[:-1] + (1,), BF16),
                              jnp.zeros(vt.shape[:-1] + (LANES - HEAD_DIM - 1,), BF16)],
                             axis=-1).reshape(b, ATT_KV_HEADS, nc, ch, LANES)
        ao = _dsa_prompt(qi.astype(BF16).transpose(0, 2, 1, 3), wi, kit,
                         (q * HEAD_DIM ** -0.5).astype(BF16).transpose(0, 2, 1, 3),
                         kh.transpose(0, 1, 2, 4, 3), vh, topk)
        ao = _heads_last(ao)
    else:
        cache_ik, cache_k2, cache_v2, page_table = sample
        past = page_table.shape[1] * PAGE_SIZE
        topk = min(TOPK_MAX, (past + t) // 4)
        rows = ATT_HEADS * t
        qi_rows = qi.astype(BF16).transpose(0, 2, 1, 3).reshape(b, rows, IDX_DIM)
        wcol = wi.transpose(0, 2, 1).reshape(b, rows, 1)
        padp = lambda z: jnp.pad(z, ((0, 0), (0, PAGE_SIZE - t), (0, 0)))
        grp = ATT_HEADS // ATT_KV_HEADS
        qh = (q * HEAD_DIM ** -0.5).astype(BF16).transpose(0, 2, 1, 3).reshape(
            b, ATT_KV_HEADS, grp * t, HEAD_DIM)
        q_bd = (qh[:, :, :, None, :] * jnp.eye(ATT_KV_HEADS, dtype=BF16)[None, :, None, :, None]
                ).reshape(b, rows, ATT_KV_W)
        acc = _dsa_sample(l, qi_rows, wcol, padp(ki), q_bd, padp(k.reshape(b, t, ATT_KV_W)),
                          padp(v.reshape(b, t, ATT_KV_W)), cache_ik, cache_k2, cache_v2, page_table, topk)
        acc = acc.reshape(b, ATT_KV_HEADS, grp, t, ATT_KV_HEADS, HEAD_DIM)
        ao = jnp.stack([acc[:, g_, :, :, g_, :] for g_ in range(ATT_KV_HEADS)], axis=1)
        ao = ao.reshape(b, ATT_HEADS, t, HEAD_DIM).transpose(0, 2, 1, 3).reshape(b, t, ATT_W)

    p = proj[..., ret_cols + att_cols:]
    p_prev = jnp.concatenate([shift0[l][:, None, :], p[:, :-1]], axis=1)
    rr, kk_, vv_, lw, aa, gg = _rwkv_pre(p, p_prev, W["rwkv_mu"][l], W["rwkv_w0"][l], W["rwkv_w2"][l],
                                         W["rwkv_a0"][l], W["rwkv_a2"][l], W["rwkv_g2"][l])
    hp = lambda z: _pad_t(z, t_rwkv)
    wo, rwkv_state = _rwkv_scan(hp(rr), hp(kk_), hp(vv_), hp(lw), hp(aa), hp(gg), rwkv_s0[l],
                                W["rwkv_kk"][l], W["rwkv_ka"][l], W["rwkv_rk"][l], W["rwkv_gn"][l],
                                RWKV_CHUNK)
    wo = wo[:, :t]

    mixed = jnp.concatenate([ro, ao, wo], axis=-1).astype(BF16).reshape(m, d)
    x2 = _matmul(mixed, W["w_out"], (l,), 1024, 512, res=x.reshape(m, d), gate=_gate_arg(g1, t, m))
    return x2, (k, v, ki, ret_state, rwkv_state, p[:, -1])


FFN_ROW_TILE = 512


def _ffn_layer(x2s, l, mls, shapes, W):
    d = x2s[0].shape[1]
    j = l // 2
    mods = [[ml[:, i * d:(i + 1) * d] for i in range(3, 6)] for ml in mls]
    if l % 2 == 0:
        outs = []
        for x2, (sh2, sc2, g2), (b, t) in zip(x2s, mods, shapes):
            h2 = _norm_mod(x2.reshape(b, t, d), W["norm_ffn"][l], sh2, sc2, BF16).reshape(b * t, d)
            act = _ffn_up(h2, W["ffn_w1"], W["ffn_w3"], (j,), 1024, 512)
            outs.append(_matmul(act, W["ffn_w2"], (j,), 512, 512, res=x2, gate=_gate_arg(g2, t, b * t)))
        return outs
    hs, routes, gates = [], [], []
    for x2, (sh2, sc2, g2), (b, t) in zip(x2s, mods, shapes):
        h2, route = _norm_mod(x2.reshape(b, t, d), W["norm_ffn"][l], sh2, sc2, F32,
                              router=(W["router_w"][j], W["router_b"][j]))
        hs.append(h2.reshape(b * t, d))
        routes.append(route.reshape(b * t, LANES))
        gates.append(jnp.repeat(g2, t, axis=0))
    n = sum(z.shape[0] for z in x2s)
    fill = -(-n // FFN_ROW_TILE) * FFN_ROW_TILE - n
    rows = lambda zs, width: jnp.concatenate(zs + [jnp.zeros((fill, width), F32)], axis=0)
    no_route = jnp.zeros((fill, LANES), F32).at[:, 0:2].set(float(N_EXPERTS))
    out = _moe(rows(hs, d), jnp.concatenate(routes + [no_route], axis=0), rows(x2s, d), rows(gates, d),
               W["moe_w1"], W["moe_w3"], W["moe_w2"], j)
    outs, start = [], 0
    for z in x2s:
        outs.append(out[start:start + z.shape[0]])
        start += z.shape[0]
    return outs


def kernel(x_prompt, x_sample, c_prompt, c_sample, cache_k, cache_v, cache_idx_k, state_ret, state_rwkv,
           state_rwkv_shift, page_table, w_ada, b_ada, norm_mix, norm_ffn, w_in, w_out, ret_gn, rwkv_mu,
           rwkv_w0, rwkv_w2, rwkv_a0, rwkv_a2, rwkv_g2, rwkv_kk, rwkv_ka, rwkv_rk, rwkv_gn, ffn_w1, ffn_w3,
           ffn_w2, router_w, router_b, moe_w1, moe_w3, moe_w2, norm_final):
    W = dict(norm_mix=norm_mix, norm_ffn=norm_ffn, w_in=w_in, w_out=w_out, ret_gn=ret_gn, rwkv_mu=rwkv_mu,
             rwkv_w0=rwkv_w0, rwkv_w2=rwkv_w2, rwkv_a0=rwkv_a0, rwkv_a2=rwkv_a2, rwkv_g2=rwkv_g2,
             rwkv_kk=rwkv_kk, rwkv_ka=rwkv_ka, rwkv_rk=rwkv_rk, rwkv_gn=rwkv_gn, ffn_w1=ffn_w1,
             ffn_w3=ffn_w3, ffn_w2=ffn_w2, router_w=router_w, router_b=router_b, moe_w1=moe_w1,
             moe_w3=moe_w3, moe_w2=moe_w2, norm_final=norm_final)
    b_p, t_p, d = x_prompt.shape
    b_s, t_s, _ = x_sample.shape
    depth = w_in.shape[0]
    npool = cache_k.shape[1]
    past = page_table.shape[1] * PAGE_SIZE

    rows = -(-(b_p + b_s) // 8) * 8
    c_all = jnp.concatenate([c_prompt, c_sample, jnp.zeros((rows - b_p - b_s, d), F32)], axis=0)
    mod = _ada_mod(c_all, w_ada, b_ada)
    mod_p = mod[:, :b_p]
    mod_s = mod[:, b_p:b_p + b_s]

    ret0 = jnp.zeros((depth, b_p, RET_HEADS, HEAD_DIM, HEAD_DIM), F32)
    rwkv0 = jnp.zeros((depth, b_p, RWKV_HEADS, HEAD_DIM, HEAD_DIM), F32)
    shift0 = jnp.zeros((depth, b_p, state_rwkv_shift.shape[-1]), F32)
    kv_rows = (depth, npool, PAGE_SIZE * ATT_KV_HEADS, HEAD_DIM)
    sample = (cache_idx_k, cache_k.reshape(kv_rows), cache_v.reshape(kv_rows), page_table)

    pos_p = jnp.arange(t_p)
    pos_s = past + jnp.arange(t_s)
    shapes = [(b_p, t_p), (b_s, t_s)]
    x_p, x_s = x_prompt, x_sample
    st_p, st_s = [], []
    for l in range(depth):
        x2_p, s_p = _mix_layer(x_p, l, mod_p[l], pos_p, ret0, rwkv0, shift0, W)
        x2_s, s_s = _mix_layer(x_s, l, mod_s[l], pos_s, state_ret, state_rwkv, state_rwkv_shift, W,
                               sample=sample)
        x3_p, x3_s = _ffn_layer([x2_p, x2_s], l, [mod_p[l], mod_s[l]], shapes, W)
        x_p = x3_p.reshape(b_p, t_p, d)
        x_s = x3_s.reshape(b_s, t_s, d)
        st_p.append(s_p)
        st_s.append(s_s)
    y_p = _norm_mod(x_p, norm_final, jnp.zeros((b_p, d), F32), jnp.zeros((b_p, d), F32), F32)
    y_s = _norm_mod(x_s, norm_final, jnp.zeros((b_s, d), F32), jnp.zeros((b_s, d), F32), F32)
    stack = lambda sts: [jnp.stack([st[i] for st in sts]) for i in range(6)]
    return (y_p, y_s, *stack(st_p), *stack(st_s))
```

```python
import functools
import math

import jax
import jax.numpy as jnp
from jax import lax
from jax.experimental import pallas as pl
from jax.experimental.pallas import tpu as pltpu

F32 = jnp.float32
BF16 = jnp.bfloat16
I32 = jnp.int32

HEAD_DIM = 64
RET_HEADS = 8
ATT_HEADS = 16
ATT_KV_HEADS = 4
RWKV_HEADS = 8
IDX_HEADS = 16
IDX_DIM = 64
RET_W = RET_HEADS * HEAD_DIM
ATT_W = ATT_HEADS * HEAD_DIM
ATT_KV_W = ATT_KV_HEADS * HEAD_DIM
RWKV_W = RWKV_HEADS * HEAD_DIM
IDX_SCALE = IDX_DIM ** -0.5 * IDX_HEADS ** -0.5
TOPK_MAX = 256
Q_BLOCK = 128
RET_CHUNK = 128
RWKV_CHUNK = 64
ROPE_THETA = 10000.0
DECAY_LORA = 96
AAA_LORA = 96
GATE_LORA = 256
PAGE_SIZE = 128
N_EXPERTS = 8
EPS = 1e-6
GN_EPS = 1e-5
RWKV_GN_EPS = 64e-5

LANES = 128
VMEM_LIMIT_BYTES = 56 * 1024 * 1024

INT_MIN = -2 ** 31
NEG_INF_KEY = INT_MIN + 0x7FFFFF
LOWEST_F32 = -3.4028234663852886e38
MASKED_LOGIT = -1e30

_NN = (((1,), (0,)), ((), ()))
_NT = (((1,), (1,)), ((), ()))
_TN = (((0,), (0,)), ((), ()))
_BNN = (((2,), (1,)), ((0,), (0,)))
_BNT = (((2,), (2,)), ((0,), (0,)))


def _cp(*sem):
    return pltpu.CompilerParams(dimension_semantics=sem, vmem_limit_bytes=VMEM_LIMIT_BYTES)


def _dot1(a, b, dims=_NN):
    return lax.dot_general(a.astype(BF16), b.astype(BF16), dims, preferred_element_type=F32)


def _split2(a):
    hi = a.astype(BF16)
    lo = (a - hi.astype(F32)).astype(BF16)
    return hi, lo


def _dot3(a, b, dims=_NN):
    ah, al = _split2(a)
    bh, bl = _split2(b)
    d = lambda x, y: lax.dot_general(x, y, dims, preferred_element_type=F32)
    return d(ah, bh) + (d(al, bh) + d(ah, bl))


def _silu(x):
    return x * jax.nn.sigmoid(x)


def _split_heads(x, nh):
    return jnp.stack([x[:, h * HEAD_DIM:(h + 1) * HEAD_DIM] for h in range(nh)], axis=0)


def _merge_heads(x):
    return jnp.concatenate([x[h] for h in range(x.shape[0])], axis=-1)


def _sort_key(s):
    bits = lax.bitcast_convert_type(s, I32)
    return jnp.where(bits < 0, bits ^ 0x7FFFFFFF, bits)


def _ada_kernel(c_ref, w_ref, b_ref, o_ref):
    a = _silu(c_ref[...]).astype(BF16)
    o_ref[...] = jnp.dot(a, w_ref[...].astype(BF16), preferred_element_type=F32) + b_ref[...]


def _ada_mod(c_all, w_ada, b_ada):
    nl, d, n6 = w_ada.shape
    r = c_all.shape[0]
    tn = 1024
    return pl.pallas_call(
        _ada_kernel,
        grid=(nl, n6 // tn),
        in_specs=[pl.BlockSpec((r, d), lambda l, j: (0, 0)),
                  pl.BlockSpec((None, d, tn), lambda l, j: (l, 0, j)),
                  pl.BlockSpec((None, 1, tn), lambda l, j: (l, 0, j))],
        out_specs=pl.BlockSpec((None, r, tn), lambda l, j: (l, 0, j)),
        out_shape=jax.ShapeDtypeStruct((nl, r, n6), F32),
        compiler_params=_cp("parallel", "parallel"),
        name="ada_mod",
    )(c_all, w_ada, b_ada.reshape(nl, 1, n6))


def _norm_rows(x_ref, g_ref, sh_ref, sc_ref):
    x = x_ref[...]
    y = x * lax.rsqrt(jnp.mean(x * x, axis=-1, keepdims=True) + EPS)
    return y * g_ref[...] * (1.0 + sc_ref[...]) + sh_ref[...]


def _norm_mod_kernel(x_ref, g_ref, sh_ref, sc_ref, o_ref):
    o_ref[...] = _norm_rows(x_ref, g_ref, sh_ref, sc_ref).astype(o_ref.dtype)


def _norm_route_kernel(x_ref, g_ref, sh_ref, sc_ref, rw_ref, rb_ref, o_ref, route_ref):
    h = _norm_rows(x_ref, g_ref, sh_ref, sc_ref)
    o_ref[...] = h
    logits = _dot3(h, rw_ref[...]) + rb_ref[...]
    lane = lax.broadcasted_iota(I32, logits.shape, 1)
    logits = jnp.where(lane < N_EXPERTS, logits, -jnp.inf)
    m1 = jnp.max(logits, axis=-1, keepdims=True)
    i1 = jnp.min(jnp.where(logits == m1, lane, LANES), axis=-1, keepdims=True)
    rest = jnp.where(lane == i1, -jnp.inf, logits)
    m2 = jnp.max(rest, axis=-1, keepdims=True)
    i2 = jnp.min(jnp.where(rest == m2, lane, LANES), axis=-1, keepdims=True)
    e2 = jnp.exp(m2 - m1)
    g1 = 1.0 / (1.0 + e2)
    g2 = e2 / (1.0 + e2)
    route_ref[...] = jnp.where(lane == 0, i1.astype(F32),
                     jnp.where(lane == 1, i2.astype(F32),
                     jnp.where(lane == 2, g1, jnp.where(lane == 3, g2, 0.0))))


def _norm_mod(x, g, shift, scale, out_dtype, router=None):
    b, t, d = x.shape
    tt = min(t, 512)
    grid = (b, t // tt)
    row_spec = pl.BlockSpec((None, tt, d), lambda i, j: (i, j, 0))
    vec_spec = pl.BlockSpec((1, d), lambda i, j: (0, 0))
    mod_spec = pl.BlockSpec((None, 1, d), lambda i, j: (i, 0, 0))
    args = [x, g.reshape(1, d), shift.reshape(b, 1, d), scale.reshape(b, 1, d)]
    if router is None:
        return pl.pallas_call(
            _norm_mod_kernel, grid=grid,
            in_specs=[row_spec, vec_spec, mod_spec, mod_spec],
            out_specs=row_spec,
            out_shape=jax.ShapeDtypeStruct((b, t, d), out_dtype),
            compiler_params=_cp("parallel", "parallel"), name="norm_mod",
        )(*args)
    rw, rb = router
    rw_pad = jnp.pad(rw, ((0, 0), (0, LANES - N_EXPERTS)))
    rb_pad = jnp.pad(rb, (0, LANES - N_EXPERTS)).reshape(1, LANES)
    return pl.pallas_call(
        _norm_route_kernel, grid=grid,
        in_specs=[row_spec, vec_spec, mod_spec, mod_spec,
                  pl.BlockSpec((d, LANES), lambda i, j: (0, 0)),
                  pl.BlockSpec((1, LANES), lambda i, j: (0, 0))],
        out_specs=[row_spec, pl.BlockSpec((None, tt, LANES), lambda i, j: (i, j, 0))],
        out_shape=[jax.ShapeDtypeStruct((b, t, d), F32), jax.ShapeDtypeStruct((b, t, LANES), F32)],
        compiler_params=_cp("parallel", "parallel"), name="norm_route",
    )(*args, rw_pad, rb_pad)


def _mm_kernel(a_ref, w_ref, o_ref):
    o_ref[...] = jnp.dot(a_ref[...], w_ref[...].astype(BF16), preferred_element_type=F32)


def _mm_res_kernel(a_ref, w_ref, x_ref, g_ref, o_ref):
    acc = jnp.dot(a_ref[...], w_ref[...].astype(BF16), preferred_element_type=F32)
    o_ref[...] = x_ref[...] + g_ref[...] * acc


def _gate_spec(gate, m, tm, tn):
    if gate.ndim == 3:
        rows = m // gate.shape[0]
        assert rows % tm == 0
        return pl.BlockSpec((None, 1, tn), lambda j, i: (i // (rows // tm), 0, j))
    return pl.BlockSpec((tm, tn), lambda j, i: (i, j))


def _matmul(a, w, lead, tm, tn, res=None, gate=None):
    m, k = a.shape
    n = w.shape[-1]
    tm = min(tm, m)
    tn = min(tn, n)
    nlead = len(lead)
    grid = (pl.cdiv(n, tn), m // tm)
    a_spec = pl.BlockSpec((tm, k), lambda j, i: (i, 0))
    w_spec = pl.BlockSpec((None,) * nlead + (k, tn), lambda j, i: lead + (0, j))
    o_spec = pl.BlockSpec((tm, tn), lambda j, i: (i, j))
    out_shape = jax.ShapeDtypeStruct((m, n), F32)
    if res is None:
        return pl.pallas_call(_mm_kernel, grid=grid, in_specs=[a_spec, w_spec], out_specs=o_spec,
                              out_shape=out_shape, compiler_params=_cp("parallel", "parallel"),
                              name="matmul")(a, w)
    return pl.pallas_call(_mm_res_kernel, grid=grid,
                          in_specs=[a_spec, w_spec, o_spec, _gate_spec(gate, m, tm, tn)],
                          out_specs=o_spec, out_shape=out_shape,
                          compiler_params=_cp("parallel", "parallel"), name="matmul_res")(a, w, res, gate)


def _ffn_up_kernel(a_ref, w1_ref, w3_ref, o_ref):
    a = a_ref[...]
    g = jnp.dot(a, w1_ref[...].astype(BF16), preferred_element_type=F32)
    u = jnp.dot(a, w3_ref[...].astype(BF16), preferred_element_type=F32)
    o_ref[...] = (_silu(g) * u).astype(BF16)


def _ffn_up(a, w1, w3, lead, tm, tf):
    m, k = a.shape
    f = w1.shape[-1]
    tm = min(tm, m)
    nlead = len(lead)
    w_spec = pl.BlockSpec((None,) * nlead + (k, tf), lambda j, i: lead + (0, j))
    return pl.pallas_call(
        _ffn_up_kernel, grid=(pl.cdiv(f, tf), m // tm),
        in_specs=[pl.BlockSpec((tm, k), lambda j, i: (i, 0)), w_spec, w_spec],
        out_specs=pl.BlockSpec((tm, tf), lambda j, i: (i, j)),
        out_shape=jax.ShapeDtypeStruct((m, f), BF16),
        compiler_params=_cp("parallel", "parallel"), name="ffn_up",
    )(a, w1, w3)


def _rope_lanes(x, cos, sin_signed):
    w = x.shape[-1]
    half = HEAD_DIM // 2
    lane = lax.broadcasted_iota(I32, x.shape, 1) % HEAD_DIM
    partner = jnp.where(lane < half, pltpu.roll(x, w - half, 1), pltpu.roll(x, half, 1))
    return x * cos + partner * sin_signed


def _ret_kernel(q_ref, k_ref, v_ref, g_ref, cos_ref, sin_ref, s0_ref, dm_ref, qd_ref, kd_ref, cd_ref, gn_ref,
                o_ref, so_ref, s_scr, *, nh):
    c = pl.program_id(1)

    @pl.when(c == 0)
    def _():
        s_scr[...] = s0_ref[...]

    cos = cos_ref[...]
    sin = sin_ref[...]
    q = _split_heads(_rope_lanes(q_ref[...], cos, sin), nh)
    k = _split_heads(_rope_lanes(k_ref[...], cos, sin) * HEAD_DIM ** -0.5, nh)
    v = _split_heads(v_ref[...], nh)
    s = s_scr[...]
    att = _dot3(q, k, _BNT) * dm_ref[...]
    out = _dot3(att, v, _BNN) + _dot3(q, s, _BNN) * qd_ref[...]
    kd = k * kd_ref[...]
    for h in range(nh):
        s_scr[h] = s[h] * cd_ref[h] + _dot3(kd[h], v[h], _TN)
    xc = out - jnp.mean(out, axis=-1, keepdims=True)
    y = xc * lax.rsqrt(jnp.mean(xc * xc, axis=-1, keepdims=True) + GN_EPS)
    o_ref[...] = _silu(g_ref[...]) * _merge_heads(y * gn_ref[...])

    @pl.when(c == pl.num_programs(1) - 1)
    def _():
        so_ref[...] = s_scr[...]


def _ret_tables(c_true, cp):
    hh = jnp.arange(RET_HEADS, dtype=F32)
    log_g = jnp.log(1.0 - 2.0 ** (-5.0 - hh))
    i = jnp.arange(cp, dtype=F32)
    rel = i[:, None] - i[None, :]
    dmask = jnp.where(rel >= 0, jnp.exp(log_g[:, None, None] * jnp.maximum(rel, 0.0)), 0.0)
    q_dec = jnp.exp(log_g[:, None] * (i[None, :] + 1.0))[..., None]
    k_dec = jnp.exp(log_g[:, None] * (c_true - 1.0 - i[None, :]))[..., None]
    c_dec = jnp.exp(log_g * c_true).reshape(RET_HEADS, 1, 1)
    return dmask, q_dec, k_dec, c_dec


def _retention(proj, cos, sin_signed, s0, gn, c_true, cp):
    b, tp, _ = proj.shape
    nh, d, w = RET_HEADS, HEAD_DIM, RET_W
    dmask, q_dec, k_dec, c_dec = _ret_tables(c_true, cp)
    col = lambda n: pl.BlockSpec((None, cp, w), lambda i, c: (i, c, n))
    rot = pl.BlockSpec((cp, w), lambda i, c: (c, 0))
    st = pl.BlockSpec((None, nh, d, d), lambda i, c: (i, 0, 0, 0))
    full = lambda shape: pl.BlockSpec(shape, lambda i, c: (0,) * len(shape))
    return pl.pallas_call(
        functools.partial(_ret_kernel, nh=nh), grid=(b, tp // cp),
        in_specs=[col(0), col(1), col(2), col(3), rot, rot, st, full((nh, cp, cp)), full((nh, cp, 1)),
                  full((nh, cp, 1)), full((nh, 1, 1)), full((nh, 1, d))],
        out_specs=[col(0), st],
        out_shape=[jax.ShapeDtypeStruct((b, tp, w), F32), jax.ShapeDtypeStruct((b, nh, d, d), F32)],
        scratch_shapes=[pltpu.VMEM((nh, d, d), F32)],
        compiler_params=_cp("parallel", "arbitrary"), name="retention",
    )(proj, proj, proj, proj, cos, sin_signed, s0, dmask, q_dec, k_dec, c_dec, gn.reshape(nh, 1, d))


def _rwkv_pre_kernel(p_ref, pp_ref, mu_ref, w0_ref, w2_ref, a0_ref, a2_ref, g2_ref,
                     r_ref, k_ref, v_ref, lw_ref, a_ref, g_ref):
    p = p_ref[...]
    xs = p + mu_ref[...] * (pp_ref[...] - p)
    w = RWKV_W
    r_ref[...] = xs[:, :w]
    k_ref[...] = xs[:, w:2 * w]
    v_ref[...] = xs[:, 2 * w:3 * w]
    tail = xs[:, 3 * w:]
    z = -(w0_ref[...] + _dot3(jnp.tanh(tail), w2_ref[...]))
    softplus = jnp.maximum(z, 0.0) + jnp.log(1.0 + jnp.exp(-jnp.abs(z)))
    lw_ref[...] = -jnp.exp(-softplus - 0.5)
    a_ref[...] = jax.nn.sigmoid(a0_ref[...] + _dot3(tail, a2_ref[...]))
    g_ref[...] = _dot3(jax.nn.sigmoid(tail), g2_ref[...])


def _rwkv_pre(p, p_prev, mu, w0, w2, a0, a2, g2):
    b, t, cols = p.shape
    w = RWKV_W
    tail = cols - 3 * w
    w2p = jnp.pad(w2, ((0, tail - DECAY_LORA), (0, 0)))
    a2p = jnp.pad(a2, ((DECAY_LORA, tail - DECAY_LORA - AAA_LORA), (0, 0)))
    g2p = jnp.pad(g2, ((DECAY_LORA + AAA_LORA, 0), (0, 0)))
    tt = min(t, 512)
    row = pl.BlockSpec((None, tt, cols), lambda i, j: (i, j, 0))
    out = pl.BlockSpec((None, tt, w), lambda i, j: (i, j, 0))
    full = lambda shape: pl.BlockSpec(shape, lambda i, j: (0,) * len(shape))
    osh = jax.ShapeDtypeStruct((b, t, w), F32)
    return pl.pallas_call(
        _rwkv_pre_kernel, grid=(b, t // tt),
        in_specs=[row, row, full((1, cols)), full((1, w)), full((tail, w)), full((1, w)),
                  full((tail, w)), full((tail, w))],
        out_specs=[out] * 6, out_shape=[osh] * 6,
        compiler_params=_cp("parallel", "parallel"), name="rwkv_pre",
    )(p, p_prev, mu.reshape(1, cols), w0.reshape(1, w), w2p, a0.reshape(1, w), a2p, g2p)


def _rwkv_kernel(r_ref, k_ref, v_ref, lw_ref, a_ref, g_ref, s0_ref, kkw_ref, kaw_ref, rkw_ref, gn_ref,
                 o_ref, so_ref, s_scr, *, nh, cs):
    c = pl.program_id(1)

    @pl.when(c == 0)
    def _():
        s_scr[...] = s0_ref[...]

    row = lax.broadcasted_iota(I32, (cs, cs), 0)
    col = lax.broadcasted_iota(I32, (cs, cs), 1)
    incl = row >= col
    strict = row > col
    ones_tril = jnp.broadcast_to(jnp.where(incl, 1.0, 0.0).astype(BF16)[None], (nh, cs, cs))
    eye = jnp.where(row == col, 1.0, 0.0)

    r = _split_heads(r_ref[...], nh)
    k = _split_heads(k_ref[...], nh)
    v = _split_heads(v_ref[...], nh)
    lw = _split_heads(lw_ref[...], nh)
    a = _split_heads(a_ref[...], nh)
    kk = k * kkw_ref[...]
    kk = kk * lax.rsqrt(jnp.sum(kk * kk, axis=-1, keepdims=True) + 1e-12)
    kmod = k * (1.0 + (a - 1.0) * kaw_ref[...])
    b = kk * a
    l1 = lw.astype(BF16)
    rem = lw - l1.astype(F32)
    l2 = rem.astype(BF16)
    l3 = (rem - l2.astype(F32)).astype(BF16)
    dd = lambda x: lax.dot_general(ones_tril, x, _BNN, preferred_element_type=F32)
    cum = dd(l1) + (dd(l2) + dd(l3))
    cum_end = cum[:, cs - 1:cs, :]
    e_inv = jnp.exp(-cum)
    e_rem = jnp.exp(cum_end - cum)
    a_t = -kk * jnp.exp(cum - lw)
    r_t = r * jnp.exp(cum)
    ar = jnp.concatenate([a_t, r_t], axis=1)
    mb = _dot3(ar, b * e_inv, _BNT)
    mk = _dot3(ar, kmod * e_inv, _BNT)
    l_ab = jnp.where(strict, mb[:, :cs], 0.0)
    p_rb = jnp.where(incl, mb[:, cs:], 0.0)
    l_ak = jnp.where(strict, mk[:, :cs], 0.0)
    p_rk = jnp.where(incl, mk[:, cs:], 0.0)
    pw = l_ab
    tinv = eye + l_ab
    for _ in range(int(math.log2(cs)) - 1):
        pw = _dot3(pw, pw, _BNN)
        tinv = tinv + _dot3(tinv, pw, _BNN)
    s0 = s_scr[...]
    hs = _dot3(ar, s0, _BNT)
    u = _dot3(tinv, hs[:, :cs] + _dot3(l_ak, v, _BNN), _BNN)
    y = hs[:, cs:] + _dot3(p_rb, u, _BNN) + _dot3(p_rk, v, _BNN)
    uv = jnp.concatenate([u, v], axis=1)
    bk = jnp.concatenate([b * e_rem, kmod * e_rem], axis=1)
    e_end = jnp.exp(cum_end)
    for h in range(nh):
        s_scr[h] = s0[h] * e_end[h] + _dot3(uv[h], bk[h], _TN)
    xc = y - jnp.mean(y, axis=-1, keepdims=True)
    yn = xc * lax.rsqrt(jnp.mean(xc * xc, axis=-1, keepdims=True) + RWKV_GN_EPS)
    bonus = jnp.sum(r * kmod * rkw_ref[...], axis=-1, keepdims=True) * v
    o_ref[...] = _merge_heads(yn * gn_ref[...] + bonus) * g_ref[...]

    @pl.when(c == pl.num_programs(1) - 1)
    def _():
        so_ref[...] = s_scr[...]


def _rwkv_scan(r, k, v, lw, a, g, s0, kkw, kaw, rkw, gn, cs):
    b, tp, w = r.shape
    nh, d = RWKV_HEADS, HEAD_DIM
    seq = pl.BlockSpec((None, cs, w), lambda i, c: (i, c, 0))
    st = pl.BlockSpec((None, nh, d, d), lambda i, c: (i, 0, 0, 0))
    par = pl.BlockSpec((nh, 1, d), lambda i, c: (0, 0, 0))
    return pl.pallas_call(
        functools.partial(_rwkv_kernel, nh=nh, cs=cs), grid=(b, tp // cs),
        in_specs=[seq] * 6 + [st] + [par] * 4,
        out_specs=[seq, st],
        out_shape=[jax.ShapeDtypeStruct((b, tp, w), F32), jax.ShapeDtypeStruct((b, nh, d, d), F32)],
        scratch_shapes=[pltpu.VMEM((nh, d, d), F32)],
        compiler_params=_cp("parallel", "arbitrary"), name="rwkv_scan",
    )(r, k, v, lw, a, g, s0, kkw.reshape(nh, 1, d), kaw.reshape(nh, 1, d), rkw.reshape(nh, 1, d),
      gn.reshape(nh, 1, d))


def _key_to_float(key):
    return lax.bitcast_convert_type(jnp.where(key < 0, key ^ 0x7FFFFFFF, key), F32)


def _kth_largest(count_ge, shape, topk):
    def bit_body(it, res):
        cand = res | lax.shift_left(jnp.int32(1), 31 - it)
        cnt = count_ge(_key_to_float(cand ^ INT_MIN))
        return jnp.where(cnt >= topk, cand, res)

    res = lax.fori_loop(0, 32, bit_body, jnp.zeros(shape, I32))
    thr = _key_to_float(res ^ INT_MIN)
    return jnp.maximum(jnp.where(thr != thr, LOWEST_F32, thr), LOWEST_F32)


def _dsa_prompt_kernel(pq_ref, pqi0_ref, pqi1_ref, cos_ref, sin_ref, wi_ref, kit_ref, kt_ref, v_ref, o_ref,
                       key_scr, bias_scr, lg_scr, m_scr, acc_scr, q_ref, qi_ref, *, ch, topk):
    i = pl.program_id(1)
    nq = Q_BLOCK
    nch = (i * nq + nq + ch - 1) // ch
    qpos = i * nq + lax.broadcasted_iota(I32, (nq, ch), 0)
    lane = lax.broadcasted_iota(I32, (nq, ch), 1)
    wi = wi_ref[...]

    cos = jnp.concatenate([cos_ref[...], cos_ref[...]], axis=1)
    sin = jnp.concatenate([sin_ref[...], sin_ref[...]], axis=1)
    q_all = _rope_lanes(pq_ref[...], cos, sin) * HEAD_DIM ** -0.5
    qi_all = _rope_lanes(jnp.concatenate([pqi0_ref[...], pqi1_ref[...]], axis=1), cos, sin)
    for h in range(ATT_HEADS):
        q_ref[h] = q_all[:, h * HEAD_DIM:(h + 1) * HEAD_DIM].astype(BF16)
    for h in range(IDX_HEADS):
        qi_ref[h] = qi_all[:, h * IDX_DIM:(h + 1) * IDX_DIM].astype(BF16)

    def score_body(j, carry):
        kit = kit_ref[j]
        s = jnp.zeros((nq, ch), F32)
        for h in range(IDX_HEADS):
            rel = jnp.dot(qi_ref[h], kit, preferred_element_type=F32)
            s = s + wi[:, h:h + 1] * jnp.maximum(rel, 0.0)
        s = s * IDX_SCALE
        s = jnp.where(s == 0.0, 0.0, s)
        s = jnp.where(j * ch + lane <= qpos, s, -jnp.inf)
        key_scr[j] = s
        return carry

    lax.fori_loop(0, nch, score_body, 0)

    def count_ge(t):
        def body(j, acc):
            m = jnp.where(key_scr[j] >= t, 1.0, 0.0)
            part = m[:, 0:LANES]
            for u in range(1, ch // LANES):
                part = part + m[:, u * LANES:(u + 1) * LANES]
            return acc + part

        acc = lax.fori_loop(0, nch, body, jnp.zeros((nq, LANES), F32))
        return jnp.sum(acc, axis=1, keepdims=True)

    thr = _kth_largest(count_ge, (nq, 1), float(topk))

    def bias_body(j, carry):
        bias_scr[j] = jnp.where(key_scr[j] >= thr, 0.0, MASKED_LOGIT)
        return carry

    lax.fori_loop(0, nch, bias_body, 0)

    hpg = ATT_HEADS // ATT_KV_HEADS
    ntile = ch // LANES
    for g in range(ATT_KV_HEADS):
        m_scr[...] = jnp.full(m_scr.shape, MASKED_LOGIT, F32)
        acc_scr[...] = jnp.zeros(acc_scr.shape, F32)

        def logit_body(j, carry, g=g):
            kt = kt_ref[g, j]
            bias = bias_scr[j]
            for hh in range(hpg):
                lg = jnp.dot(q_ref[g * hpg + hh], kt, preferred_element_type=F32) + bias
                lg_scr[hh, j] = lg
                part = lg[:, 0:LANES]
                for u in range(1, ntile):
                    part = jnp.maximum(part, lg[:, u * LANES:(u + 1) * LANES])
                m_scr[hh] = jnp.maximum(m_scr[hh], part)
            return carry

        lax.fori_loop(0, nch, logit_body, 0)
        for hh in range(hpg):
            m_scr[hh] = jnp.broadcast_to(jnp.max(m_scr[hh], axis=-1, keepdims=True), (nq, LANES))

        def pv_body(j, carry, g=g):
            vv = v_ref[g, j]
            for hh in range(hpg):
                m = m_scr[hh]
                lg = lg_scr[hh, j]
                p = jnp.concatenate(
                    [jnp.exp(lg[:, u * LANES:(u + 1) * LANES] - m).astype(BF16) for u in range(ntile)], axis=1)
                acc_scr[hh] += jnp.dot(p, vv, preferred_element_type=F32)
            return carry

        lax.fori_loop(0, nch, pv_body, 0)
        outs = []
        for hh in range(hpg):
            acc = acc_scr[hh]
            outs.append(acc[:, :HEAD_DIM] / acc[:, HEAD_DIM:HEAD_DIM + 1])
        o_ref[:, g * hpg * HEAD_DIM:(g + 1) * hpg * HEAD_DIM] = jnp.concatenate(outs, axis=1)


Q_COL = 4 * RET_W
QI_COL = Q_COL + ATT_W + 2 * ATT_KV_W


def _dsa_prompt(proj, cos, sin_signed, wi, kit, kt, v_c, topk):
    b, t, _ = proj.shape
    nh, d = ATT_HEADS, HEAD_DIM
    nc, ch = kit.shape[1], kit.shape[3]
    hpg = ATT_HEADS // ATT_KV_HEADS
    half = IDX_HEADS * IDX_DIM // 2
    rot = pl.BlockSpec((Q_BLOCK, RET_W), lambda i, j: (j, 0))
    return pl.pallas_call(
        functools.partial(_dsa_prompt_kernel, ch=ch, topk=topk), grid=(b, t // Q_BLOCK),
        in_specs=[pl.BlockSpec((None, Q_BLOCK, ATT_W), lambda i, j: (i, j, Q_COL // ATT_W)),
                  pl.BlockSpec((None, Q_BLOCK, half), lambda i, j: (i, j, QI_COL // half)),
                  pl.BlockSpec((None, Q_BLOCK, half), lambda i, j: (i, j, QI_COL // half + 1)),
                  rot, rot,
                  pl.BlockSpec((None, Q_BLOCK, IDX_HEADS), lambda i, j: (i, j, 0)),
                  pl.BlockSpec((None, nc, IDX_DIM, ch), lambda i, j: (i, 0, 0, 0)),
                  pl.BlockSpec((None, ATT_KV_HEADS, nc, d, ch), lambda i, j: (i, 0, 0, 0, 0)),
                  pl.BlockSpec((None, ATT_KV_HEADS, nc, ch, LANES), lambda i, j: (i, 0, 0, 0, 0))],
        out_specs=pl.BlockSpec((None, Q_BLOCK, ATT_W), lambda i, j: (i, j, 0)),
        out_shape=jax.ShapeDtypeStruct((b, t, ATT_W), F32),
        scratch_shapes=[pltpu.VMEM((nc, Q_BLOCK, ch), F32), pltpu.VMEM((nc, Q_BLOCK, ch), F32),
                        pltpu.VMEM((hpg, nc, Q_BLOCK, ch), F32), pltpu.VMEM((hpg, Q_BLOCK, LANES), F32),
                        pltpu.VMEM((hpg, Q_BLOCK, LANES), F32), pltpu.VMEM((nh, Q_BLOCK, d), BF16),
                        pltpu.VMEM((IDX_HEADS, Q_BLOCK, IDX_DIM), BF16)],
        compiler_params=_cp("parallel", "arbitrary"), name="dsa_prompt",
    )(proj, proj, proj, cos, sin_signed, wi, kit, kt, v_c)


def _dsa_s_score_kernel(pt_ref, qi_ref, w_ref, *rest, pg, ns, tq, past):
    kc_refs = rest[:pg]
    kn_ref = rest[pg]
    o_ref = rest[pg + 1]
    j = pl.program_id(1)
    qi = qi_ref[...]
    wcol = w_ref[...]
    qpos = past + lax.broadcasted_iota(I32, (tq, PAGE_SIZE), 0)
    lane = lax.broadcasted_iota(I32, (tq, PAGE_SIZE), 1)

    def page_keys(kpage, kbase):
        rel = jnp.maximum(_dot1(qi, kpage, _NT), 0.0) * wcol
        s = jnp.sum(rel.reshape(IDX_HEADS, tq, PAGE_SIZE), axis=0) * IDX_SCALE
        s = jnp.where(s == 0.0, 0.0, s)
        s = jnp.where(kbase + lane <= qpos, s, -jnp.inf)
        return s

    @pl.when(j < ns)
    def _():
        for u in range(pg):
            o_ref[u] = page_keys(kc_refs[u][...], (j * pg + u) * PAGE_SIZE)

    @pl.when(j == ns)
    def _():
        o_ref[0] = page_keys(kn_ref[...], past)
        for u in range(1, pg):
            o_ref[u] = jnp.full((tq, PAGE_SIZE), -jnp.inf, F32)


def _dsa_s_attn_kernel(pt_ref, keys_ref, q_ref, *rest, pg, ns, tq, topk):
    kc_refs = rest[:pg]
    vc_refs = rest[pg:2 * pg]
    kn_ref, vn_ref, o_ref, thr_scr, m_scr, l_scr, acc_scr = rest[2 * pg:]
    j = pl.program_id(1)

    @pl.when(j == 0)
    def _():
        def count_ge(t):
            m = jnp.where(keys_ref[...] >= t[None], 1.0, 0.0)
            return jnp.sum(jnp.sum(m, axis=0), axis=1, keepdims=True)

        thr = _kth_largest(count_ge, (tq, 1), float(topk))
        thr_scr[...] = jnp.broadcast_to(thr, thr_scr.shape)
        m_scr[...] = jnp.full(m_scr.shape, MASKED_LOGIT, F32)
        l_scr[...] = jnp.zeros(l_scr.shape, F32)
        acc_scr[...] = jnp.zeros(acc_scr.shape, F32)

    q = q_ref[...]
    thr = thr_scr[...]

    def pages(slots, kps, vps):
        lgs = []
        for slot, kp in zip(slots, kps):
            bias8 = jnp.where(keys_ref[slot] >= thr, 0.0, MASKED_LOGIT)
            bias = jnp.broadcast_to(bias8[None], (ATT_HEADS, tq, PAGE_SIZE)).reshape(ATT_HEADS * tq, PAGE_SIZE)
            lgs.append(_dot1(q, kp, _NT) + bias)
        lg = lgs[0] if len(lgs) == 1 else jnp.concatenate(lgs, axis=1)
        m_old = m_scr[...]
        m_new = jnp.maximum(m_old, jnp.max(lg, axis=-1, keepdims=True))
        alpha = jnp.exp(m_old - m_new)
        p = jnp.exp(lg - m_new)
        l_scr[...] = alpha * l_scr[...] + jnp.sum(p, axis=-1, keepdims=True)
        acc = alpha * acc_scr[...]
        for n, vp in enumerate(vps):
            acc = acc + _dot1(p[:, n * PAGE_SIZE:(n + 1) * PAGE_SIZE], vp)
        acc_scr[...] = acc
        m_scr[...] = m_new

    @pl.when(j < ns)
    def _():
        pages([j * pg + u for u in range(pg)], [r[...] for r in kc_refs], [r[...] for r in vc_refs])

    @pl.when(j == ns)
    def _():
        pages([ns * pg], [kn_ref[...]], [vn_ref[...]])
        o_ref[...] = acc_scr[...] / l_scr[...]


def _dsa_sample(l, qi_rows, wcol, ki_new, q_bd, k_new, v_new, cache_ik, cache_k2, cache_v2, page_table, topk):
    b, rows, _ = qi_rows.shape
    tq = rows // IDX_HEADS
    npages = page_table.shape[1]
    pg = 8
    ns = npages // pg
    past = npages * PAGE_SIZE
    kvw = ATT_KV_W

    def cache_spec(u, width):
        return pl.BlockSpec((None, None, PAGE_SIZE, width),
                            lambda i, j, pt: (l, pt[i, jnp.minimum(j, ns - 1) * pg + u], 0, 0))

    per_b = lambda shape: pl.BlockSpec((None,) + shape, lambda i, j, pt: (i,) + (0,) * len(shape))
    keys = pl.pallas_call(
        functools.partial(_dsa_s_score_kernel, pg=pg, ns=ns, tq=tq, past=past),
        grid_spec=pltpu.PrefetchScalarGridSpec(
            num_scalar_prefetch=1, grid=(b, ns + 1),
            in_specs=[per_b((rows, IDX_DIM)), per_b((rows, 1))]
                     + [cache_spec(u, IDX_DIM) for u in range(pg)] + [per_b((PAGE_SIZE, IDX_DIM))],
            out_specs=pl.BlockSpec((None, pg, tq, PAGE_SIZE), lambda i, j, pt: (i, j, 0, 0))),
        out_shape=jax.ShapeDtypeStruct((b, (ns + 1) * pg, tq, PAGE_SIZE), F32),
        compiler_params=_cp("parallel", "arbitrary"), name="dsa_sample_score",
    )(page_table, qi_rows, wcol, *([cache_ik] * pg), ki_new)

    nslots = (ns + 1) * pg
    return pl.pallas_call(
        functools.partial(_dsa_s_attn_kernel, pg=pg, ns=ns, tq=tq, topk=topk),
        grid_spec=pltpu.PrefetchScalarGridSpec(
            num_scalar_prefetch=1, grid=(b, ns + 1),
            in_specs=[per_b((nslots, tq, PAGE_SIZE)), per_b((rows, kvw))]
                     + [cache_spec(u, kvw) for u in range(pg)] + [cache_spec(u, kvw) for u in range(pg)]
                     + [per_b((PAGE_SIZE, kvw)), per_b((PAGE_SIZE, kvw))],
            out_specs=per_b((rows, kvw)),
            scratch_shapes=[pltpu.VMEM((tq, PAGE_SIZE), F32), pltpu.VMEM((rows, 1), F32),
                            pltpu.VMEM((rows, 1), F32), pltpu.VMEM((rows, kvw), F32)]),
        out_shape=jax.ShapeDtypeStruct((b, rows, kvw), F32),
        compiler_params=_cp("parallel", "arbitrary"), name="dsa_sample_attn",
    )(page_table, keys, q_bd, *([cache_k2] * pg), *([cache_v2] * pg), k_new, v_new)


MOE_TM = 512
GATHER_ROWS = 256


def _row_copy(src_ref, row, buf, r, sem):
    return pltpu.make_async_copy(src_ref.at[pl.ds(row, 1), :], buf.at[pl.ds(r, 1), :], sem)


def _gather_kernel(idx_ref, src_ref, o_ref, buf, sem, *, nrows):
    base = pl.program_id(0) * nrows

    def start(r, carry):
        _row_copy(src_ref, idx_ref[base + r], buf, r, sem).start()
        return carry

    def wait(r, carry):
        _row_copy(src_ref, idx_ref[base + r], buf, r, sem).wait()
        return carry

    lax.fori_loop(0, nrows, start, 0)
    lax.fori_loop(0, nrows, wait, 0)
    o_ref[...] = buf[...].astype(o_ref.dtype)


def _gather_rows(src, idx, out_dtype):
    p = idx.shape[0]
    d = src.shape[1]
    nrows = GATHER_ROWS
    return pl.pallas_call(
        functools.partial(_gather_kernel, nrows=nrows),
        grid_spec=pltpu.PrefetchScalarGridSpec(
            num_scalar_prefetch=1, grid=(p // nrows,),
            in_specs=[pl.BlockSpec(memory_space=pl.ANY)],
            out_specs=pl.BlockSpec((nrows, d), lambda i, idx: (i, 0)),
            scratch_shapes=[pltpu.VMEM((nrows, d), src.dtype), pltpu.SemaphoreType.DMA]),
        out_shape=jax.ShapeDtypeStruct((p, d), out_dtype),
        compiler_params=_cp("arbitrary"), name="moe_gather",
    )(idx, src)


def _moe_up_kernel(te_ref, nu_ref, a_ref, w1_ref, w3_ref, o_ref):
    used = pl.program_id(1) < nu_ref[0]

    @pl.when(used)
    def _():
        a = a_ref[...]
        g = jnp.dot(a, w1_ref[...].astype(BF16), preferred_element_type=F32)
        u = jnp.dot(a, w3_ref[...].astype(BF16), preferred_element_type=F32)
        o_ref[...] = (_silu(g) * u).astype(BF16)

    @pl.when(jnp.logical_not(used))
    def _():
        o_ref[...] = jnp.zeros(o_ref.shape, BF16)


def _moe_down_kernel(te_ref, nu_ref, a_ref, w_ref, o_ref):
    used = pl.program_id(1) < nu_ref[0]

    @pl.when(used)
    def _():
        o_ref[...] = jnp.dot(a_ref[...], w_ref[...].astype(BF16), preferred_element_type=F32)

    @pl.when(jnp.logical_not(used))
    def _():
        o_ref[...] = jnp.zeros(o_ref.shape, F32)


def _combine_kernel(d1_ref, d2_ref, y_ref, x_ref, route_ref, gate_ref, o_ref, buf1, buf2, sem, *, nrows):
    base = pl.program_id(0) * nrows

    def start(r, carry):
        _row_copy(y_ref, d1_ref[base + r], buf1, r, sem).start()
        _row_copy(y_ref, d2_ref[base + r], buf2, r, sem).start()
        return carry

    def wait(r, carry):
        _row_copy(y_ref, d1_ref[base + r], buf1, r, sem).wait()
        _row_copy(y_ref, d2_ref[base + r], buf2, r, sem).wait()
        return carry

    lax.fori_loop(0, nrows, start, 0)
    lax.fori_loop(0, nrows, wait, 0)
    route = route_ref[...]
    f = route[:, 2:3] * buf1[...] + route[:, 3:4] * buf2[...]
    o_ref[...] = x_ref[...] + gate_ref[...] * f


def _moe(h, route, x, gate, w1, w3, w2, jm):
    n, d = h.shape
    f = w1.shape[-1]
    tm = MOE_TM
    n_tiles = (2 * n + N_EXPERTS * (tm - 1)) // tm
    p_max = n_tiles * tm

    e_pair = jnp.concatenate([route[:, 0], route[:, 1]]).astype(I32)
    onehot = (e_pair[:, None] == jnp.arange(N_EXPERTS, dtype=I32)[None, :]).astype(I32)
    running = jnp.cumsum(onehot, axis=0)
    counts = running[-1]
    rank = jnp.sum(onehot * (running - 1), axis=1)
    tiles_per = (counts + tm - 1) // tm
    tile_end = jnp.cumsum(tiles_per)
    pad_start = (tile_end - tiles_per) * tm
    dest = (jnp.sum(onehot * pad_start[None, :], axis=1) + rank).astype(I32)
    dest = jnp.where(e_pair < N_EXPERTS, dest, p_max)
    src_token = jnp.zeros((p_max,), I32).at[dest].set(jnp.arange(2 * n, dtype=I32) % n, mode="drop")
    dest = jnp.minimum(dest, p_max - 1)
    n_used = tile_end[-1:].astype(I32)
    tile_expert = jnp.minimum(
        jnp.sum((jnp.arange(n_tiles, dtype=I32)[:, None] >= tile_end[None, :]).astype(I32), axis=1),
        N_EXPERTS - 1).astype(I32)

    hs = _gather_rows(h, src_token, BF16)

    tf = 1024
    lim = lambda i, nu: jnp.minimum(i, nu[0] - 1)
    act = pl.pallas_call(
        _moe_up_kernel,
        grid_spec=pltpu.PrefetchScalarGridSpec(
            num_scalar_prefetch=2, grid=(f // tf, n_tiles),
            in_specs=[pl.BlockSpec((tm, d), lambda j, i, te, nu: (lim(i, nu), 0)),
                      pl.BlockSpec((None, None, d, tf), lambda j, i, te, nu: (jm, te[lim(i, nu)], 0, j)),
                      pl.BlockSpec((None, None, d, tf), lambda j, i, te, nu: (jm, te[lim(i, nu)], 0, j))],
            out_specs=pl.BlockSpec((tm, tf), lambda j, i, te, nu: (i, j))),
        out_shape=jax.ShapeDtypeStruct((p_max, f), BF16),
        compiler_params=_cp("arbitrary", "arbitrary"), name="moe_up",
    )(tile_expert, n_used, hs, w1, w3)

    tn = min(512, d)
    y = pl.pallas_call(
        _moe_down_kernel,
        grid_spec=pltpu.PrefetchScalarGridSpec(
            num_scalar_prefetch=2, grid=(d // tn, n_tiles),
            in_specs=[pl.BlockSpec((tm, f), lambda j, i, te, nu: (lim(i, nu), 0)),
                      pl.BlockSpec((None, None, f, tn), lambda j, i, te, nu: (jm, te[lim(i, nu)], 0, j))],
            out_specs=pl.BlockSpec((tm, tn), lambda j, i, te, nu: (i, j))),
        out_shape=jax.ShapeDtypeStruct((p_max, d), F32),
        compiler_params=_cp("arbitrary", "arbitrary"), name="moe_down",
    )(tile_expert, n_used, act, w2)

    nrows = min(GATHER_ROWS, n)
    if gate.ndim == 3:
        per = n // gate.shape[0] // nrows
        gate_spec = pl.BlockSpec((None, 1, d), lambda i, d1, d2: (i // per, 0, 0))
    else:
        gate_spec = pl.BlockSpec((nrows, d), lambda i, d1, d2: (i, 0))
    row_spec = pl.BlockSpec((nrows, d), lambda i, d1, d2: (i, 0))
    return pl.pallas_call(
        functools.partial(_combine_kernel, nrows=nrows),
        grid_spec=pltpu.PrefetchScalarGridSpec(
            num_scalar_prefetch=2, grid=(n // nrows,),
            in_specs=[pl.BlockSpec(memory_space=pl.ANY), row_spec,
                      pl.BlockSpec((nrows, LANES), lambda i, d1, d2: (i, 0)), gate_spec],
            out_specs=row_spec,
            scratch_shapes=[pltpu.VMEM((nrows, d), F32), pltpu.VMEM((nrows, d), F32),
                            pltpu.SemaphoreType.DMA]),
        out_shape=jax.ShapeDtypeStruct((n, d), F32),
        compiler_params=_cp("arbitrary"), name="moe_combine",
    )(dest[:n], dest[n:], y, x, route, gate)


def _rope_tables(pos):
    half = HEAD_DIM // 2
    inv = ROPE_THETA ** (-jnp.arange(half, dtype=F32) / half)
    ang = pos.astype(F32)[:, None] * inv[None, :]
    return jnp.cos(ang)[:, None, :], jnp.sin(ang)[:, None, :]


def _rope(x, cos, sin):
    half = x.shape[-1] // 2
    x1 = x[..., :half]
    x2 = x[..., half:]
    return jnp.concatenate([x1 * cos - x2 * sin, x2 * cos + x1 * sin], axis=-1)


def _heads_first(x, nh):
    b, t, _ = x.shape
    return x.reshape(b, t, nh, HEAD_DIM).transpose(0, 2, 1, 3)


def _heads_last(x):
    b, nh, t, d = x.shape
    return x.transpose(0, 2, 1, 3).reshape(b, t, nh * d)


def _pad_t(x, tp):
    t = x.shape[1]
    return x if t == tp else jnp.pad(x, ((0, 0), (0, tp - t), (0, 0)))


def _gate_arg(g, t, m):
    if t >= 256:
        return g[:, None, :]
    return jnp.repeat(g, t, axis=0)


def _mix_layer(x, l, ml, pos, ret_s0, rwkv_s0, shift0, W, sample=None):
    b, t, d = x.shape
    m = b * t
    cos, sin = _rope_tables(pos)
    ret_c = math.gcd(t, RET_CHUNK)
    ret_cp = RET_CHUNK
    t_ret = -(-t // ret_cp) * ret_cp
    t_rwkv = -(-t // RWKV_CHUNK) * RWKV_CHUNK
    ret_cols = 4 * RET_W
    att_cols = ATT_W + 2 * ATT_KV_W + IDX_HEADS * IDX_DIM + IDX_DIM + IDX_HEADS
    sh1, sc1, g1 = [ml[:, i * d:(i + 1) * d] for i in range(3)]
    h = _norm_mod(x, W["norm_mix"][l], sh1, sc1, BF16)
    proj = _matmul(h.reshape(m, d), W["w_in"], (l,), 1024, 768).reshape(b, t, -1)

    cos_w = jnp.tile(jnp.concatenate([cos[:, 0], cos[:, 0]], axis=-1), (1, RET_HEADS))
    sin_w = jnp.tile(jnp.concatenate([-sin[:, 0], sin[:, 0]], axis=-1), (1, RET_HEADS))
    pad_rows = lambda z: z if t == t_ret else jnp.pad(z, ((0, t_ret - t), (0, 0)))
    ro, ret_state = _retention(_pad_t(proj, t_ret), pad_rows(cos_w), pad_rows(sin_w), ret_s0[l],
                               W["ret_gn"][l], ret_c, ret_cp)
    ro = ro[:, :t]

    o = ret_cols + ATT_W
    k = _rope(proj[..., o:o + ATT_KV_W].reshape(b, t, ATT_KV_HEADS, HEAD_DIM), cos, sin)
    o += ATT_KV_W
    v = proj[..., o:o + ATT_KV_W].reshape(b, t, ATT_KV_HEADS, HEAD_DIM)
    o += ATT_KV_W + IDX_HEADS * IDX_DIM
    ki = _rope(proj[..., o:o + IDX_DIM].reshape(b, t, 1, IDX_DIM), cos, sin)[:, :, 0]
    o += IDX_DIM
    wi = proj[..., o:o + IDX_HEADS]
    if sample is None:
        topk = min(TOPK_MAX, t // 4)
        ch = min(512, t)
        nc = t // ch
        kit = ki.astype(BF16).reshape(b, nc, ch, IDX_DIM).transpose(0, 1, 3, 2)
        kh = k.astype(BF16).transpose(0, 2, 1, 3).reshape(b, ATT_KV_HEADS, nc, ch, HEAD_DIM)
        vt = v.astype(BF16).transpose(0, 2, 1, 3)
        vh = jnp.concatenate([vt, jnp.ones(vt.shape[:-1] + (1,), BF16),
                              jnp.zeros(vt.shape[:-1] + (LANES - HEAD_DIM - 1,), BF16)],
                             axis=-1).reshape(b, ATT_KV_HEADS, nc, ch, LANES)
        ao = _dsa_prompt(proj, cos_w, sin_w, wi, kit, kh.transpose(0, 1, 2, 4, 3), vh, topk)
    else:
        q = _rope(proj[..., Q_COL:Q_COL + ATT_W].reshape(b, t, ATT_HEADS, HEAD_DIM), cos, sin)
        qi = _rope(proj[..., QI_COL:QI_COL + IDX_HEADS * IDX_DIM].reshape(b, t, IDX_HEADS, IDX_DIM), cos, sin)
        cache_ik, cache_k2, cache_v2, page_table = sample
        past = page_table.shape[1] * PAGE_SIZE
        topk = min(TOPK_MAX, (past + t) // 4)
        rows = ATT_HEADS * t
        qi_rows = qi.astype(BF16).transpose(0, 2, 1, 3).reshape(b, rows, IDX_DIM)
        wcol = wi.transpose(0, 2, 1).reshape(b, rows, 1)
        padp = lambda z: jnp.pad(z, ((0, 0), (0, PAGE_SIZE - t), (0, 0)))
        grp = ATT_HEADS // ATT_KV_HEADS
        qh = (q * HEAD_DIM ** -0.5).astype(BF16).transpose(0, 2, 1, 3).reshape(
            b, ATT_KV_HEADS, grp * t, HEAD_DIM)
        q_bd = (qh[:, :, :, None, :] * jnp.eye(ATT_KV_HEADS, dtype=BF16)[None, :, None, :, None]
                ).reshape(b, rows, ATT_KV_W)
        acc = _dsa_sample(l, qi_rows, wcol, padp(ki), q_bd, padp(k.reshape(b, t, ATT_KV_W)),
                          padp(v.reshape(b, t, ATT_KV_W)), cache_ik, cache_k2, cache_v2, page_table, topk)
        acc = acc.reshape(b, ATT_KV_HEADS, grp, t, ATT_KV_HEADS, HEAD_DIM)
        ao = jnp.stack([acc[:, g_, :, :, g_, :] for g_ in range(ATT_KV_HEADS)], axis=1)
        ao = ao.reshape(b, ATT_HEADS, t, HEAD_DIM).transpose(0, 2, 1, 3).reshape(b, t, ATT_W)

    p = proj[..., ret_cols + att_cols:]
    p_prev = jnp.concatenate([shift0[l][:, None, :], p[:, :-1]], axis=1)
    rr, kk_, vv_, lw, aa, gg = _rwkv_pre(p, p_prev, W["rwkv_mu"][l], W["rwkv_w0"][l], W["rwkv_w2"][l],
                                         W["rwkv_a0"][l], W["rwkv_a2"][l], W["rwkv_g2"][l])
    hp = lambda z: _pad_t(z, t_rwkv)
    wo, rwkv_state = _rwkv_scan(hp(rr), hp(kk_), hp(vv_), hp(lw), hp(aa), hp(gg), rwkv_s0[l],
                                W["rwkv_kk"][l], W["rwkv_ka"][l], W["rwkv_rk"][l], W["rwkv_gn"][l],
                                RWKV_CHUNK)
    wo = wo[:, :t]

    mixed = jnp.concatenate([ro, ao, wo], axis=-1).astype(BF16).reshape(m, d)
    x2 = _matmul(mixed, W["w_out"], (l,), 1024, 512, res=x.reshape(m, d), gate=_gate_arg(g1, t, m))
    return x2, (k, v, ki, ret_state, rwkv_state, p[:, -1])


FFN_ROW_TILE = 512


def _ffn_layer(x2s, l, mls, shapes, W):
    d = x2s[0].shape[1]
    j = l // 2
    mods = [[ml[:, i * d:(i + 1) * d] for i in range(3, 6)] for ml in mls]
    if l % 2 == 0:
        outs = []
        for x2, (sh2, sc2, g2), (b, t) in zip(x2s, mods, shapes):
            h2 = _norm_mod(x2.reshape(b, t, d), W["norm_ffn"][l], sh2, sc2, BF16).reshape(b * t, d)
            act = _ffn_up(h2, W["ffn_w1"], W["ffn_w3"], (j,), 1024, 512)
            outs.append(_matmul(act, W["ffn_w2"], (j,), 512, 512, res=x2, gate=_gate_arg(g2, t, b * t)))
        return outs
    hs, routes, gates = [], [], []
    for x2, (sh2, sc2, g2), (b, t) in zip(x2s, mods, shapes):
        h2, route = _norm_mod(x2.reshape(b, t, d), W["norm_ffn"][l], sh2, sc2, F32,
                              router=(W["router_w"][j], W["router_b"][j]))
        hs.append(h2.reshape(b * t, d))
        routes.append(route.reshape(b * t, LANES))
        gates.append(jnp.repeat(g2, t, axis=0))
    n = sum(z.shape[0] for z in x2s)
    fill = -(-n // FFN_ROW_TILE) * FFN_ROW_TILE - n
    rows = lambda zs, width: jnp.concatenate(zs + [jnp.zeros((fill, width), F32)], axis=0)
    no_route = jnp.zeros((fill, LANES), F32).at[:, 0:2].set(float(N_EXPERTS))
    out = _moe(rows(hs, d), jnp.concatenate(routes + [no_route], axis=0), rows(x2s, d), rows(gates, d),
               W["moe_w1"], W["moe_w3"], W["moe_w2"], j)
    outs, start = [], 0
    for z in x2s:
        outs.append(out[start:start + z.shape[0]])
        start += z.shape[0]
    return outs


def kernel(x_prompt, x_sample, c_prompt, c_sample, cache_k, cache_v, cache_idx_k, state_ret, state_rwkv,
           state_rwkv_shift, page_table, w_ada, b_ada, norm_mix, norm_ffn, w_in, w_out, ret_gn, rwkv_mu,
           rwkv_w0, rwkv_w2, rwkv_a0, rwkv_a2, rwkv_g2, rwkv_kk, rwkv_ka, rwkv_rk, rwkv_gn, ffn_w1, ffn_w3,
           ffn_w2, router_w, router_b, moe_w1, moe_w3, moe_w2, norm_final):
    W = dict(norm_mix=norm_mix, norm_ffn=norm_ffn, w_in=w_in, w_out=w_out, ret_gn=ret_gn, rwkv_mu=rwkv_mu,
             rwkv_w0=rwkv_w0, rwkv_w2=rwkv_w2, rwkv_a0=rwkv_a0, rwkv_a2=rwkv_a2, rwkv_g2=rwkv_g2,
             rwkv_kk=rwkv_kk, rwkv_ka=rwkv_ka, rwkv_rk=rwkv_rk, rwkv_gn=rwkv_gn, ffn_w1=ffn_w1,
             ffn_w3=ffn_w3, ffn_w2=ffn_w2, router_w=router_w, router_b=router_b, moe_w1=moe_w1,
             moe_w3=moe_w3, moe_w2=moe_w2, norm_final=norm_final)
    b_p, t_p, d = x_prompt.shape
    b_s, t_s, _ = x_sample.shape
    depth = w_in.shape[0]
    npool = cache_k.shape[1]
    past = page_table.shape[1] * PAGE_SIZE

    rows = -(-(b_p + b_s) // 8) * 8
    c_all = jnp.concatenate([c_prompt, c_sample, jnp.zeros((rows - b_p - b_s, d), F32)], axis=0)
    mod = _ada_mod(c_all, w_ada, b_ada)
    mod_p = mod[:, :b_p]
    mod_s = mod[:, b_p:b_p + b_s]

    ret0 = jnp.zeros((depth, b_p, RET_HEADS, HEAD_DIM, HEAD_DIM), F32)
    rwkv0 = jnp.zeros((depth, b_p, RWKV_HEADS, HEAD_DIM, HEAD_DIM), F32)
    shift0 = jnp.zeros((depth, b_p, state_rwkv_shift.shape[-1]), F32)
    kv_flat = (depth, npool, PAGE_SIZE, ATT_KV_W)
    sample = (cache_idx_k.astype(BF16), cache_k.astype(BF16).reshape(kv_flat),
              cache_v.astype(BF16).reshape(kv_flat), page_table)

    pos_p = jnp.arange(t_p)
    pos_s = past + jnp.arange(t_s)
    shapes = [(b_p, t_p), (b_s, t_s)]
    x_p, x_s = x_prompt, x_sample
    st_p, st_s = [], []
    for l in range(depth):
        x2_p, s_p = _mix_layer(x_p, l, mod_p[l], pos_p, ret0, rwkv0, shift0, W)
        x2_s, s_s = _mix_layer(x_s, l, mod_s[l], pos_s, state_ret, state_rwkv, state_rwkv_shift, W,
                               sample=sample)
        x3_p, x3_s = _ffn_layer([x2_p, x2_s], l, [mod_p[l], mod_s[l]], shapes, W)
        x_p = x3_p.reshape(b_p, t_p, d)
        x_s = x3_s.reshape(b_s, t_s, d)
        st_p.append(s_p)
        st_s.append(s_s)
    y_p = _norm_mod(x_p, norm_final, jnp.zeros((b_p, d), F32), jnp.zeros((b_p, d), F32), F32)
    y_s = _norm_mod(x_s, norm_final, jnp.zeros((b_s, d), F32), jnp.zeros((b_s, d), F32), F32)
    stack = lambda sts: [jnp.stack([st[i] for st in sts]) for i in range(6)]
    return (y_p, y_s, *stack(st_p), *stack(st_s))
```
